```python
import jax, jax.numpy as jnp
from jax import lax
import numpy as np

D_MODEL = 2048
BATCH = 2
SEQ = 4096
DEPTH = 4
DEC_BATCH = 128
DEC_SEQ = 1
PAST_LEN = 8192
PAGE_SIZE = 128

N_META = 16
N_A = DEPTH // 2
N_B = DEPTH - N_A
HEAD_A = 64
H_A = D_MODEL // HEAD_A
LORA_DECAY = 96
LORA_AAA = 96
LORA_MV = 64
LORA_GATE = 256
GN_EPS = 64e-5
H_B = D_MODEL // 128
NOPE_DIM = 128
ROPE_DIM = 64
V_DIM = 128
QK_DIM = NOPE_DIM + ROPE_DIM
Q_RANK = 512
KV_RANK = 512
ROPE_THETA = 10000.0
Q_BLOCK = 128
D_FF = 11 * D_MODEL // 4
CONV_W = 3
RMS_EPS = 1e-6

kernel_name = 'rwkv7_mla_yoco_convffn_step'

F32 = jnp.float32


def rmsnorm(x, g, eps=RMS_EPS):
    xf = x.astype(F32)
    y = xf * lax.rsqrt(jnp.mean(xf * xf, axis=-1, keepdims=True) + eps)
    return (y * g.astype(F32)).astype(x.dtype)


def qk_norm(x, g):
    gp = g[NOPE_DIM:]
    return rmsnorm(x, jnp.concatenate([g[:NOPE_DIM], gp, gp]))


def rope_tables(pos):
    inv = ROPE_THETA ** (-jnp.arange(0, ROPE_DIM, 2, dtype=F32) / ROPE_DIM)
    ang = pos.astype(F32)[:, None] * inv[None]
    return jnp.cos(ang), jnp.sin(ang)


def apply_rope(x, cos, sin):
    half = ROPE_DIM // 2
    x1, x2 = x[..., :half].astype(F32), x[..., half:].astype(F32)
    return jnp.concatenate([x1 * cos - x2 * sin, x2 * cos + x1 * sin], axis=-1).astype(x.dtype)


def wkv7_scan(r, w, k, v, a, b, s0):
    seq = tuple(jnp.moveaxis(t.astype(F32), 1, 0) for t in (r, w, k, v, a, b))

    def step(S, inp):
        r_t, w_t, k_t, v_t, a_t, b_t = inp
        sa = jnp.einsum('bhij,bhj->bhi', S, a_t)
        S = S * w_t[:, :, None, :] + sa[..., None] * b_t[:, :, None, :] + v_t[..., None] * k_t[:, :, None, :]
        return S, jnp.einsum('bhij,bhj->bhi', S, r_t)

    S, y = lax.scan(step, s0.astype(F32), seq)
    return jnp.moveaxis(y, 0, 1), S


def rwkv7_time_mix(xn, shift_prev, s0, v_first, vres, mu, w_rkv, w_o, w0, w1, w2, a0, a1, a2,
                   g1, g2, k_k, k_a, r_k, lnx_w, lnx_b):
    B, T, D = xn.shape
    x_prev = jnp.concatenate([shift_prev[:, None].astype(xn.dtype), xn[:, :-1]], axis=1)
    xs = xn[None] + (x_prev - xn)[None] * mu[:, None, None, :]
    r, k, v = jnp.einsum('sbtd,sde->sbte', xs[:3], w_rkv)
    xw, xa, xg = xs[3], xs[4], xs[5]
    w_log = -jax.nn.softplus(-(w0 + jnp.tanh(xw @ w1) @ w2).astype(F32)) - 0.5
    decay = jnp.exp(-jnp.exp(w_log))
    if vres is not None:
        v0, v1, v2 = vres
        v = v + (v_first - v) * jax.nn.sigmoid(v0 + (xs[2] @ v1) @ v2)
    a = jax.nn.sigmoid(a0 + (xa @ a1) @ a2)
    g = jax.nn.sigmoid(xg @ g1) @ g2
    hd = lambda t: t.reshape(B, T, H_A, HEAD_A)
    kk = hd(k * k_k).astype(F32)
    kk = kk / jnp.maximum(jnp.linalg.norm(kk, axis=-1, keepdims=True), 1e-12)
    k = k * (1.0 + (a - 1.0) * k_a)
    r_h, k_h, v_h, a_h = hd(r), hd(k), hd(v), hd(a)
    y, s_new = wkv7_scan(r_h, hd(decay), k_h, v_h, -kk, kk * a_h, s0)
    mean = jnp.mean(y, axis=-1, keepdims=True)
    var = jnp.mean(jnp.square(y - mean), axis=-1, keepdims=True)
    yn = ((y - mean) * lax.rsqrt(var + GN_EPS)).reshape(B, T, D) * lnx_w + lnx_b
    bonus = jnp.sum((r_h * k_h * r_k).astype(F32), axis=-1, keepdims=True) * v_h
    out = ((yn + bonus.reshape(B, T, D)) * g) @ w_o
    return out.astype(xn.dtype), xn[:, -1], s_new, v


def conv_ffn(xn, buf, w_in, conv_w, conv_b, w_out):
    T = xn.shape[1]
    u = xn @ w_in
    c, z = u[..., :D_FF], u[..., D_FF:]
    cf = jnp.concatenate([buf.astype(c.dtype), c], axis=1)
    conv = conv_b + cf[:, 0:T] * conv_w[0] + cf[:, 1:T + 1] * conv_w[1] + cf[:, 2:T + 2] * conv_w[2]
    return (jax.nn.silu(conv) * z) @ w_out, cf[:, -(CONV_W - 1):]


def latent_kv(h, pos, norm_kv, w_dkv, g_ckv):
    u = rmsnorm(h, norm_kv) @ w_dkv
    ckv = rmsnorm(u[..., :KV_RANK], g_ckv)
    cos, sin = rope_tables(pos)
    kpe = apply_rope(u[..., KV_RANK:], cos[None], sin[None])
    return ckv, kpe


def expand_kv(ckv, kpe, w_ukv, g_k):
    kv = jnp.einsum('blr,rhe->blhe', ckv, w_ukv)
    k_nope, v = kv[..., :NOPE_DIM], kv[..., NOPE_DIM:]
    kpe_h = jnp.broadcast_to(kpe[:, :, None, :], k_nope.shape[:-1] + (ROPE_DIM,)).astype(k_nope.dtype)
    return qk_norm(jnp.concatenate([k_nope, kpe_h], axis=-1), g_k), v


def mla_queries(xn, pos, w_dq, g_q, w_uq, g_qn):
    cq = rmsnorm(xn @ w_dq, g_q)
    q = jnp.einsum('btr,rhe->bthe', cq, w_uq)
    cos, sin = rope_tables(pos)
    q_pe = apply_rope(q[..., NOPE_DIM:], cos[None, :, None], sin[None, :, None])
    return qk_norm(jnp.concatenate([q[..., :NOPE_DIM], q_pe], axis=-1), g_qn)


def prompt_attention(q, k, v):
    B, T = q.shape[:2]
    nq = -(-T // Q_BLOCK)
    qb = jnp.pad(q, ((0, 0), (0, nq * Q_BLOCK - T), (0, 0), (0, 0)))
    qb = qb.reshape(B, nq, Q_BLOCK, H_B, QK_DIM).transpose(1, 0, 2, 3, 4)
    kpos = jnp.arange(T)
    scale = QK_DIM ** -0.5

    def one(args):
        qblk, i = args
        qpos = i * Q_BLOCK + jnp.arange(Q_BLOCK)
        s = jnp.einsum('bqhe,bkhe->bhqk', qblk, k).astype(F32) * scale
        s = jnp.where(kpos[None, :] <= qpos[:, None], s, -jnp.inf)
        p = jax.nn.softmax(s, axis=-1).astype(v.dtype)
        return jnp.einsum('bhqk,bkhe->bqhe', p, v)

    o = lax.map(one, (qb, jnp.arange(nq)))
    return o.transpose(1, 0, 2, 3, 4).reshape(B, nq * Q_BLOCK, H_B, V_DIM)[:, :T]


def online_update(carry, s, v):
    m, l, acc = carry
    m_new = jnp.maximum(m, jnp.max(s, axis=-1))
    alpha = jnp.exp(m - m_new)
    p = jnp.exp(s - m_new[..., None])
    l = l * alpha + jnp.sum(p, axis=-1)
    acc = acc * alpha[..., None] + jnp.einsum('bhsk,bkhe->bhse', p, v.astype(F32))
    return (m_new, l, acc)


def sample_attention(q, cache_ckv, cache_kpe, page_table, ckv_new, kpe_new, w_ukv, g_k):
    DB, S = q.shape[:2]
    scale = QK_DIM ** -0.5
    init = (jnp.full((DB, H_B, S), -jnp.inf, F32), jnp.zeros((DB, H_B, S), F32),
            jnp.zeros((DB, H_B, S, V_DIM), F32))

    def page_step(carry, phys):
        k, v = expand_kv(cache_ckv[phys], cache_kpe[phys], w_ukv, g_k)
        s = jnp.einsum('bshe,bkhe->bhsk', q, k).astype(F32) * scale
        return online_update(carry, s, v), None

    carry, _ = lax.scan(page_step, init, page_table.T)
    k, v = expand_kv(ckv_new, kpe_new, w_ukv, g_k)
    s = jnp.einsum('bshe,bkhe->bhsk', q, k).astype(F32) * scale
    s = jnp.where(jnp.tril(jnp.ones((S, S), bool)), s, -jnp.inf)
    m, l, acc = online_update(carry, s, v)
    return (acc / l[..., None]).transpose(0, 2, 1, 3).astype(q.dtype)


def setup_inputs(seed: int = 0) -> dict:
    key = jax.random.key(seed)
    ks = iter(jax.random.split(key, 64))
    nrm = lambda shape, scale: jax.random.normal(next(ks), shape, F32) * scale
    gain = lambda shape: 1.0 + nrm(shape, 0.02)
    uni = lambda shape, lo, hi: jax.random.uniform(next(ks), shape, F32, lo, hi)
    D = D_MODEL
    n_pages = PAST_LEN // PAGE_SIZE
    n_used = DEC_BATCH * n_pages
    n_pool = n_used + max(1, n_used // 4)
    page_table = jax.random.permutation(next(ks), n_pool)[:n_used].reshape(DEC_BATCH, n_pages).astype(jnp.int32)
    return {
        'x_prompt': nrm((BATCH, SEQ, D), 1.0),
        'x_sample': nrm((DEC_BATCH, DEC_SEQ, D), 1.0),
        'state_shift': nrm((N_A, DEC_BATCH, D), 1.0),
        'state_wkv': nrm((N_A, DEC_BATCH, H_A, HEAD_A, HEAD_A), 0.3),
        'state_conv': nrm((DEPTH, DEC_BATCH, CONV_W - 1, D_FF), 1.0),
        'cache_ckv': nrm((n_pool, PAGE_SIZE, KV_RANK), 1.0),
        'cache_kpe': nrm((n_pool, PAGE_SIZE, ROPE_DIM), 1.0),
        'page_table': page_table,
        'meta_tokens': nrm((N_META, D), 1.0),
        'norm_mix': gain((DEPTH, D)),
        'norm_ffn': gain((DEPTH, D)),
        'mu': uni((N_A, 6, D), 0.0, 1.0),
        'w_rkv': nrm((N_A, 3, D, D), D ** -0.5),
        'w_o_a': nrm((N_A, D, D), D ** -0.5),
        'w0': uni((N_A, D), -6.0, -1.0),
        'w1': nrm((N_A, D, LORA_DECAY), D ** -0.5),
        'w2': nrm((N_A, LORA_DECAY, D), 0.5 * LORA_DECAY ** -0.5),
        'a0': nrm((N_A, D), 0.1),
        'a1': nrm((N_A, D, LORA_AAA), D ** -0.5),
        'a2': nrm((N_A, LORA_AAA, D), 0.5 * LORA_AAA ** -0.5),
        'v0': nrm((N_A - 1, D), 0.1),
        'v1': nrm((N_A - 1, D, LORA_MV), D ** -0.5),
        'v2': nrm((N_A - 1, LORA_MV, D), 0.5 * LORA_MV ** -0.5),
        'g1': nrm((N_A, D, LORA_GATE), D ** -0.5),
        'g2': nrm((N_A, LORA_GATE, D), LORA_GATE ** -0.5),
        'k_k': 0.85 + nrm((N_A, D), 0.02),
        'k_a': gain((N_A, D)),
        'r_k': nrm((N_A, H_A, HEAD_A), 0.1),
        'lnx_w': gain((N_A, D)),
        'lnx_b': nrm((N_A, D), 0.01),
        'ffn_w_in': nrm((DEPTH, D, 2 * D_FF), D ** -0.5),
        'ffn_conv_w': nrm((DEPTH, CONV_W, D_FF), CONV_W ** -0.5),
        'ffn_conv_b': nrm((DEPTH, D_FF), 0.01),
        'ffn_w_out': nrm((DEPTH, D_FF, D), D_FF ** -0.5),
        'norm_kv': gain((D,)),
        'w_dkv': nrm((D, KV_RANK + ROPE_DIM), D ** -0.5),
        'g_ckv': gain((KV_RANK,)),
        'w_ukv': nrm((KV_RANK, H_B, NOPE_DIM + V_DIM), KV_RANK ** -0.5),
        'g_k': gain((NOPE_DIM + ROPE_DIM // 2,)),
        'w_dq': nrm((N_B, D, Q_RANK), D ** -0.5),
        'g_q': gain((N_B, Q_RANK)),
        'w_uq': nrm((N_B, Q_RANK, H_B, QK_DIM), Q_RANK ** -0.5),
        'g_qn': gain((N_B, NOPE_DIM + ROPE_DIM // 2)),
        'w_o_b': nrm((N_B, H_B, V_DIM, D), (H_B * V_DIM) ** -0.5),
    }


def reference(x_prompt, x_sample, state_shift, state_wkv, state_conv, cache_ckv, cache_kpe, page_table,
              meta_tokens, norm_mix, norm_ffn, mu, w_rkv, w_o_a, w0, w1, w2, a0, a1, a2, v0, v1, v2,
              g1, g2, k_k, k_a, r_k, lnx_w, lnx_b, ffn_w_in, ffn_conv_w, ffn_conv_b, ffn_w_out,
              norm_kv, w_dkv, g_ckv, w_ukv, g_k, w_dq, g_q, w_uq, g_qn, w_o_b):

    def trunk(h, pos, shift0, wkv0, conv0, prepare, attend):
        new_shift, new_wkv, new_conv = [], [], []
        v_first = None
        ckv = kpe = kv_ctx = None
        for i in range(DEPTH):
            xn = rmsnorm(h, norm_mix[i])
            if i < N_A:
                vres = None if i == 0 else (v0[i - 1], v1[i - 1], v2[i - 1])
                o, sh, S, v = rwkv7_time_mix(xn, shift0[i], wkv0[i], v_first, vres, mu[i], w_rkv[i], w_o_a[i],
                                             w0[i], w1[i], w2[i], a0[i], a1[i], a2[i], g1[i], g2[i],
                                             k_k[i], k_a[i], r_k[i], lnx_w[i], lnx_b[i])
                if i == 0:
                    v_first = v
                new_shift.append(sh)
                new_wkv.append(S)
            else:
                j = i - N_A
                q = mla_queries(xn, pos, w_dq[j], g_q[j], w_uq[j], g_qn[j])
                o = jnp.einsum('bthe,hed->btd', attend(q, kv_ctx), w_o_b[j])
            h = h + o
            f, cb = conv_ffn(rmsnorm(h, norm_ffn[i]), conv0[i], ffn_w_in[i], ffn_conv_w[i], ffn_conv_b[i], ffn_w_out[i])
            h = h + f
            new_conv.append(cb)
            if i == N_A - 1:
                ckv, kpe = latent_kv(h, pos, norm_kv, w_dkv, g_ckv)
                kv_ctx = prepare(ckv, kpe)
        return h, jnp.stack(new_shift), jnp.stack(new_wkv), jnp.stack(new_conv), ckv, kpe

    B, _, D = x_prompt.shape
    h0 = jnp.concatenate([jnp.broadcast_to(meta_tokens[None].astype(x_prompt.dtype), (B, N_META, D)), x_prompt], axis=1)
    T = h0.shape[1]
    hp, shift_p, wkv_p, conv_p, ckv_p, kpe_p = trunk(
        h0, jnp.arange(T),
        jnp.zeros((N_A, B, D), x_prompt.dtype),
        jnp.zeros((N_A, B, H_A, HEAD_A, HEAD_A), F32),
        jnp.zeros((DEPTH, B, CONV_W - 1, D_FF), x_prompt.dtype),
        lambda c, p: expand_kv(c, p, w_ukv, g_k),
        lambda q, kv: prompt_attention(q, kv[0], kv[1]))
    y_prompt = hp[:, N_META:]

    past_len = page_table.shape[1] * cache_ckv.shape[1]
    pos_s = past_len + jnp.arange(x_sample.shape[1])
    y_sample, shift_s, wkv_s, conv_s, ckv_s, kpe_s = trunk(
        x_sample, pos_s, state_shift, state_wkv, state_conv,
        lambda c, p: (c, p),
        lambda q, kv: sample_attention(q, cache_ckv, cache_kpe, page_table, kv[0], kv[1], w_ukv, g_k))

    return (y_prompt, y_sample, shift_p, wkv_p, conv_p, ckv_p, kpe_p, shift_s, wkv_s, conv_s, ckv_s, kpe_s)
```

```python
import functools

import numpy as np
import jax
import jax.numpy as jnp
from jax import lax
from jax.experimental import pallas as pl
from jax.experimental.pallas import tpu as pltpu

F32 = jnp.float32
BF16 = jnp.bfloat16

RMS_EPS = 1e-6
ROPE_THETA = 10000.0
GN_EPS_PER_CHANNEL = 1e-5
WKV_CHUNK = 64
WKV_LANES = 256
ROW_ALIGN = 64
ATTN_ALIGN = 128
VMEM_LIMIT = 56 * 1024 * 1024


def _nt(a, b):
    return lax.dot_general(a, b, (((1,), (1,)), ((), ())), preferred_element_type=F32)


def _tn(a, b):
    return lax.dot_general(a, b, (((0,), (0,)), ((), ())), preferred_element_type=F32)


def _nn(a, b):
    return jnp.dot(a, b, preferred_element_type=F32)


def _pick(n, candidates):
    for c in candidates:
        if n % c == 0:
            return c
    return n


def _mm_kernel(x_ref, w_ref, *rest, act, has_res):
    if has_res:
        r_ref, o_ref, wb_ref = rest
    else:
        o_ref, wb_ref = rest

    @pl.when(pl.program_id(1) == 0)
    def _():
        wb_ref[...] = w_ref[...].astype(BF16)

    acc = _nn(x_ref[...].astype(BF16), wb_ref[...])
    if act is not None:
        acc = act(acc)
    if has_res:
        acc = acc + r_ref[...]
    o_ref[...] = acc.astype(o_ref.dtype)


def _mm(x, w, *, act=None, residual=None, out_dtype=F32):
    M, K = x.shape
    N = w.shape[1]
    tm = _pick(M, (768, 512, 384, 256, 128))
    if K > 4096:
        tm = _pick(M, (384, 256, 128))
    tn = _pick(N, (1024, 512, 256, 128)) if K <= 2048 else _pick(N, (512, 256, 128))
    grid = (N // tn, M // tm)
    in_specs = [pl.BlockSpec((tm, K), lambda j, i: (i, 0)),
                pl.BlockSpec((K, tn), lambda j, i: (0, j))]
    args = [x, w]
    if residual is not None:
        in_specs.append(pl.BlockSpec((tm, tn), lambda j, i: (i, j)))
        args.append(residual)
    return pl.pallas_call(
        functools.partial(_mm_kernel, act=act, has_res=residual is not None),
        grid=grid,
        in_specs=in_specs,
        out_specs=pl.BlockSpec((tm, tn), lambda j, i: (i, j)),
        out_shape=jax.ShapeDtypeStruct((M, N), out_dtype),
        scratch_shapes=[pltpu.VMEM((K, tn), BF16)],
        compiler_params=pltpu.CompilerParams(
            dimension_semantics=("arbitrary", "arbitrary"), vmem_limit_bytes=VMEM_LIMIT),
    )(*args)


def _bmm_kernel(x_ref, w_ref, o_ref):
    o_ref[0] = _nn(x_ref[0].astype(BF16), w_ref[0].astype(BF16))


def _bmm(x, w):
    G, M, K = x.shape
    N = w.shape[2]
    return pl.pallas_call(
        _bmm_kernel,
        grid=(G,),
        in_specs=[pl.BlockSpec((1, M, K), lambda g: (g, 0, 0)),
                  pl.BlockSpec((1, K, N), lambda g: (g, 0, 0))],
        out_specs=pl.BlockSpec((1, M, N), lambda g: (g, 0, 0)),
        out_shape=jax.ShapeDtypeStruct((G, M, N), F32),
    )(x, w)


def _rms_kernel(x_ref, g_ref, o_ref):
    x = x_ref[...].astype(F32)
    y = x * lax.rsqrt(jnp.mean(x * x, axis=-1, keepdims=True) + RMS_EPS)
    o_ref[...] = (y * g_ref[...]).astype(o_ref.dtype)


def _rmsnorm(x, g, out_dtype):
    M, D = x.shape
    tm = _pick(M, (768, 512, 384, 256, 128))
    return pl.pallas_call(
        _rms_kernel,
        grid=(M // tm,),
        in_specs=[pl.BlockSpec((tm, D), lambda i: (i, 0)),
                  pl.BlockSpec((1, D), lambda i: (0, 0))],
        out_specs=pl.BlockSpec((tm, D), lambda i: (i, 0)),
        out_shape=jax.ShapeDtypeStruct((M, D), out_dtype),
    )(x, g.reshape(1, D).astype(F32))


def _wkv_chunk_kernel(r_ref, lw_ref, k_ref, v_ref, a_ref, b_ref,
                      bd_ref, ts_ref, ti_ref, tri_ref, eye_ref,
                      y_ref, s_ref, st_ref, *, chunk, groups, t_valid, n_chunks, levels):
    c = pl.program_id(1)

    @pl.when(c == 0)
    def _():
        st_ref[...] = jnp.zeros_like(st_ref)

    row = c * chunk + lax.broadcasted_iota(jnp.int32, (chunk, 1), 0)
    valid = row < t_valid
    r, lw, k, v, a, b = (jnp.where(valid, ref[0], 0.0)
                         for ref in (r_ref, lw_ref, k_ref, v_ref, a_ref, b_ref))

    tri = tri_ref[...]
    hi = lw.astype(BF16)
    rem = lw - hi.astype(F32)
    mid = rem.astype(BF16)
    lo = (rem - mid.astype(F32)).astype(BF16)
    lc = _nn(tri, hi) + _nn(tri, mid) + _nn(tri, lo)
    lc_end = lc[chunk - 1:chunk, :]
    e_neg = jnp.exp(-lc)
    e_end = jnp.exp(lc_end - lc)
    rt = r * jnp.exp(lc)
    at = a * jnp.exp(lc - lw)
    kt = k * e_neg
    bt = b * e_neg
    kh = k * e_end
    bh = b * e_end
    d_end = jnp.exp(lc_end)

    bd = bd_ref[...]

    def stack(x):
        return jnp.concatenate([x] * groups, axis=0) * bd

    rt_f = stack(rt)
    rt_s, at_s, kt_s, bt_s, kh_s, bh_s, v_s = (
        stack(x).astype(BF16) for x in (rt, at, kt, bt, kh, bh, v))
    rt_s = rt_f.astype(BF16)

    ts = ts_ref[...] > 0.0
    ti = ti_ref[...] > 0.0
    m_ab = jnp.where(ts, _nt(at_s, bt_s), 0.0)
    m_ak = jnp.where(ts, _nt(at_s, kt_s), 0.0)
    n_rb = jnp.where(ti, _nt(rt_s, bt_s), 0.0).astype(BF16)
    n_rk = jnp.where(ti, _nt(rt_s, kt_s), 0.0).astype(BF16)

    eye = eye_ref[...]
    inv = eye + m_ab
    pw = m_ab
    for _ in range(levels):
        pwb = pw.astype(BF16)
        pw = _nn(pwb, pwb)
        inv = inv + _nn(inv.astype(BF16), pw.astype(BF16))
    inv_b = inv.astype(BF16)

    p_b = _nn(inv_b, at_s).astype(BF16)
    q_b = _nn(inv_b, _nn(m_ak.astype(BF16), v_s).astype(BF16)).astype(BF16)

    g_mat = eye * d_end + _tn(bh_s, p_b)
    f_mat = _tn(bh_s, q_b) + _tn(kh_s, v_s)
    r_y = rt_f + _nn(n_rb, p_b)
    y_0 = _nn(n_rb, q_b) + _nn(n_rk, v_s)

    st = st_ref[...]
    st_b = st.astype(BF16)
    y_s = _nn(r_y.astype(BF16), st_b) + y_0
    y = y_s[0:chunk]
    for h in range(1, groups):
        y = y + y_s[h * chunk:(h + 1) * chunk]
    y_ref[0] = y
    st_ref[...] = _nn(g_mat.astype(BF16), st_b) + f_mat

    @pl.when(c == n_chunks - 1)
    def _():
        s_ref[0, 0] = st_ref[...]


def _wkv_chunked(r, lw, k, v, a, b, t_valid, head):
    B, Tp, D = r.shape
    L = WKV_CHUNK
    W = WKV_LANES
    G = W // head
    ng = D // W
    nc = Tp // L
    assert Tp % L == 0 and D % W == 0 and G * L == W
    idx = np.arange(W)
    same = (idx[:, None] // L) == (idx[None, :] // L)
    bd = same.astype(np.float32)
    ts = (same & (idx[None, :] < idx[:, None])).astype(np.float32)
    ti = (same & (idx[None, :] <= idx[:, None])).astype(np.float32)
    tri = jnp.asarray(np.tril(np.ones((L, L), np.float32)), BF16)
    eye = np.eye(W, dtype=np.float32)
    levels = int(np.log2(L)) - 1
    assert 2 ** (levels + 1) == L

    seq = pl.BlockSpec((1, L, W), lambda u, c: (u // ng, c, u % ng))
    const = lambda shape: pl.BlockSpec(shape, lambda u, c: (0, 0))
    y, st = pl.pallas_call(
        functools.partial(_wkv_chunk_kernel, chunk=L, groups=G, t_valid=t_valid, n_chunks=nc, levels=levels),
        grid=(B * ng, nc),
        in_specs=[seq] * 6 + [const((W, W))] * 3 + [const((L, L)), const((W, W))],
        out_specs=[seq, pl.BlockSpec((1, 1, W, W), lambda u, c: (u // ng, u % ng, 0, 0))],
        out_shape=[jax.ShapeDtypeStruct((B, Tp, D), F32),
                   jax.ShapeDtypeStruct((B, ng, W, W), F32)],
        scratch_shapes=[pltpu.VMEM((W, W), F32)],
        compiler_params=pltpu.CompilerParams(dimension_semantics=("arbitrary", "arbitrary")),
    )(r, lw, k, v, a, b, jnp.asarray(bd), jnp.asarray(ts), jnp.asarray(ti), tri, jnp.asarray(eye))
    st = st.reshape(B, ng, G, head, G, head)
    st = jnp.stack([st[:, :, h, :, h, :] for h in range(G)], axis=2)
    return y, jnp.swapaxes(st, -1, -2).reshape(B, ng * G, head, head)


def _wkv_step_kernel(r_ref, w_ref, k_ref, v_ref, a_ref, b_ref, s_ref, y_ref, so_ref, *, heads, head):
    ii = lax.broadcasted_iota(jnp.int32, (head, head), 0)
    jj = lax.broadcasted_iota(jnp.int32, (head, head), 1)
    eye = ii == jj
    for h in range(heads):
        row = lambda ref: ref[0, h:h + 1, :]
        s = s_ref[0, h]
        sa = jnp.sum(s * row(a_ref), axis=-1, keepdims=True)
        v_col = jnp.sum(jnp.where(eye, row(v_ref), 0.0), axis=-1, keepdims=True)
        s_new = s * row(w_ref) + sa * row(b_ref) + v_col * row(k_ref)
        so_ref[0, h] = s_new
        y_col = jnp.sum(s_new * row(r_ref), axis=-1, keepdims=True)
        y_ref[0, h:h + 1, :] = jnp.sum(jnp.where(eye, y_col, 0.0), axis=0, keepdims=True)


def _wkv_step(r, w, k, v, a, b, s0):
    B, H, N = r.shape
    vec = pl.BlockSpec((1, H, N), lambda i: (i, 0, 0))
    mat = pl.BlockSpec((1, H, N, N), lambda i: (i, 0, 0, 0))
    return pl.pallas_call(
        functools.partial(_wkv_step_kernel, heads=H, head=N),
        grid=(B,),
        in_specs=[vec] * 6 + [mat],
        out_specs=[vec, mat],
        out_shape=[jax.ShapeDtypeStruct((B, H, N), F32), jax.ShapeDtypeStruct((B, H, N, N), F32)],
    )(r, w, k, v, a, b, s0)


def _flash_kernel(qi_ref, ki_ref, last_ref, q_ref, k_ref, v_ref, o_ref, m_sc, l_sc, acc_sc, *, tq, tk):
    p = pl.program_id(2)
    qi = qi_ref[p]
    ki = ki_ref[p]

    @pl.when(ki == 0)
    def _():
        m_sc[...] = jnp.full_like(m_sc, -jnp.inf)
        l_sc[...] = jnp.zeros_like(l_sc)
        acc_sc[...] = jnp.zeros_like(acc_sc)

    s = _nt(q_ref[0, 0], k_ref[0, 0])
    qpos = qi * tq + lax.broadcasted_iota(jnp.int32, (tq, 1), 0)
    kpos = ki * tk + lax.broadcasted_iota(jnp.int32, (1, tk), 1)
    s = jnp.where(kpos <= qpos, s, -jnp.inf)
    m_old = m_sc[...]
    m_new = jnp.maximum(m_old, jnp.max(s, axis=-1, keepdims=True))
    alpha = jnp.exp(m_old - m_new)
    pm = jnp.exp(s - m_new)
    l_sc[...] = alpha * l_sc[...] + jnp.sum(pm, axis=-1, keepdims=True)
    acc_sc[...] = alpha * acc_sc[...] + _nn(pm.astype(BF16), v_ref[0, 0])
    m_sc[...] = m_new

    @pl.when(last_ref[p] == 1)
    def _():
        o_ref[0] = (acc_sc[...] / l_sc[...]).astype(o_ref.dtype)


def _flash(q, k, v, tq, tk):
    B, H, T, E = q.shape
    V = v.shape[-1]
    pairs = [(qi, ki) for qi in range(T // tq) for ki in range(T // tk) if ki * tk <= qi * tq + tq - 1]
    qi_tab = np.array([p[0] for p in pairs], np.int32)
    ki_tab = np.array([p[1] for p in pairs], np.int32)
    last = np.array([1 if (i + 1 == len(pairs) or pairs[i + 1][0] != pairs[i][0]) else 0
                     for i in range(len(pairs))], np.int32)
    grid_spec = pltpu.PrefetchScalarGridSpec(
        num_scalar_prefetch=3,
        grid=(B, H, len(pairs)),
        in_specs=[pl.BlockSpec((1, 1, tq, E), lambda b, h, p, qt, kt, lt: (b, h, qt[p], 0)),
                  pl.BlockSpec((1, 1, tk, E), lambda b, h, p, qt, kt, lt: (b, h, kt[p], 0)),
                  pl.BlockSpec((1, 1, tk, V), lambda b, h, p, qt, kt, lt: (b, h, kt[p], 0))],
        out_specs=pl.BlockSpec((1, tq, V), lambda b, h, p, qt, kt, lt: (b, qt[p], h)),
        scratch_shapes=[pltpu.VMEM((tq, 1), F32), pltpu.VMEM((tq, 1), F32), pltpu.VMEM((tq, V), F32)],
    )
    return pl.pallas_call(
        functools.partial(_flash_kernel, tq=tq, tk=tk),
        grid_spec=grid_spec,
        out_shape=jax.ShapeDtypeStruct((B, T, H * V), BF16),
        compiler_params=pltpu.CompilerParams(
            dimension_semantics=("arbitrary", "arbitrary", "arbitrary"), vmem_limit_bytes=VMEM_LIMIT),
    )(jnp.asarray(qi_tab), jnp.asarray(ki_tab), jnp.asarray(last), q, k, v)


def _paged_kernel(pt_ref, c0_ref, c1_ref, p0_ref, p1_ref, qa_ref, qp_ref, wnt_ref,
                  acc_ref, m_ref, l_ref, m_sc, l_sc, acc_sc, *, heads, nope, qk_dim, n_steps):
    step = pl.program_id(1)

    @pl.when(step == 0)
    def _():
        m_sc[...] = jnp.full_like(m_sc, -jnp.inf)
        l_sc[...] = jnp.zeros_like(l_sc)
        acc_sc[...] = jnp.zeros_like(acc_sc)

    c = jnp.concatenate([c0_ref[0], c1_ref[0]], axis=0).astype(BF16)
    kp = jnp.concatenate([p0_ref[0], p1_ref[0]], axis=0)
    tokens = c.shape[0]
    knt = _nt(wnt_ref[...], c)
    ssq = jnp.sum((knt * knt).reshape(heads, nope, tokens), axis=1)
    ones = jnp.ones((8, kp.shape[1]), BF16)
    kss = _nt(ones, (kp * kp).astype(BF16))[0:1, :]
    rinv = lax.rsqrt((ssq + kss) * (1.0 / qk_dim) + RMS_EPS)
    s = (_nt(qa_ref[0].astype(BF16), c) + _nt(qp_ref[0].astype(BF16), kp.astype(BF16))) * rinv
    m_old = m_sc[...]
    m_new = jnp.maximum(m_old, jnp.max(s, axis=-1, keepdims=True))
    alpha = jnp.exp(m_old - m_new)
    pm = jnp.exp(s - m_new)
    l_sc[...] = alpha * l_sc[...] + jnp.sum(pm, axis=-1, keepdims=True)
    acc_sc[...] = alpha * acc_sc[...] + _nn(pm.astype(BF16), c)
    m_sc[...] = m_new

    @pl.when(step == n_steps - 1)
    def _():
        acc_ref[0] = acc_sc[...]
        m_ref[0] = jnp.broadcast_to(m_sc[...], m_ref.shape[1:])
        l_ref[0] = jnp.broadcast_to(l_sc[...], l_ref.shape[1:])


def _paged_attn(page_table, cache_ckv, cache_kpe, qa, qp, wnt, nope):
    DB, n_pages = page_table.shape
    _, page, R = cache_ckv.shape
    rope = cache_kpe.shape[-1]
    H = qa.shape[1]
    assert n_pages % 2 == 0
    n_steps = n_pages // 2
    grid_spec = pltpu.PrefetchScalarGridSpec(
        num_scalar_prefetch=1,
        grid=(DB, n_steps),
        in_specs=[pl.BlockSpec((1, page, R), lambda b, s, pt: (pt[b, 2 * s], 0, 0)),
                  pl.BlockSpec((1, page, R), lambda b, s, pt: (pt[b, 2 * s + 1], 0, 0)),
                  pl.BlockSpec((1, page, rope), lambda b, s, pt: (pt[b, 2 * s], 0, 0)),
                  pl.BlockSpec((1, page, rope), lambda b, s, pt: (pt[b, 2 * s + 1], 0, 0)),
                  pl.BlockSpec((1, H, R), lambda b, s, pt: (b, 0, 0)),
                  pl.BlockSpec((1, H, rope), lambda b, s, pt: (b, 0, 0)),
                  pl.BlockSpec(wnt.shape, lambda b, s, pt: (0, 0))],
        out_specs=[pl.BlockSpec((1, H, R), lambda b, s, pt: (b, 0, 0)),
                   pl.BlockSpec((1, H, 128), lambda b, s, pt: (b, 0, 0)),
                   pl.BlockSpec((1, H, 128), lambda b, s, pt: (b, 0, 0))],
        scratch_shapes=[pltpu.VMEM((H, 1), F32), pltpu.VMEM((H, 1), F32), pltpu.VMEM((H, R), F32)],
    )
    acc, m, l = pl.pallas_call(
        functools.partial(_paged_kernel, heads=H, nope=nope, qk_dim=nope + rope, n_steps=n_steps),
        grid_spec=grid_spec,
        out_shape=[jax.ShapeDtypeStruct((DB, H, R), F32),
                   jax.ShapeDtypeStruct((DB, H, 128), F32),
                   jax.ShapeDtypeStruct((DB, H, 128), F32)],
        compiler_params=pltpu.CompilerParams(dimension_semantics=("arbitrary", "arbitrary")),
    )(page_table, cache_ckv, cache_ckv, cache_kpe, cache_kpe, qa, qp, wnt)
    return acc, m[:, :, 0], l[:, :, 0]


def _pad_cols(w, n):
    return jnp.pad(w, ((0, 0), (0, n - w.shape[1])))


def _pad_rows(w, n):
    return jnp.pad(w, ((0, n - w.shape[0]), (0, 0)))


def _lora(x, w1, w2, act):
    rank = w1.shape[1]
    rp = -(-rank // 128) * 128
    mid = _mm(x, _pad_cols(w1, rp), act=act, out_dtype=BF16)
    return _mm(mid, _pad_rows(w2, rp))


def _rope(x, cos, sin):
    half = x.shape[-1] // 2
    x1, x2 = x[..., :half], x[..., half:]
    return jnp.concatenate([x1 * cos - x2 * sin, x2 * cos + x1 * sin], axis=-1)


def _head_rms(x, g):
    return x * lax.rsqrt(jnp.mean(x * x, axis=-1, keepdims=True) + RMS_EPS) * g


def kernel(x_prompt, x_sample, state_shift, state_wkv, state_conv, cache_ckv, cache_kpe, page_table, meta_tokens, norm_mix, norm_ffn, mu, w_rkv, w_o_a, w0, w1, w2, a0, a1, a2, v0, v1, v2, g1, g2, k_k, k_a, r_k, lnx_w, lnx_b, ffn_w_in, ffn_conv_w, ffn_conv_b, ffn_w_out, norm_kv, w_dkv, g_ckv, w_ukv, g_k, w_dq, g_q, w_uq, g_qn, w_o_b):
    B, seq, D = x_prompt.shape
    DB = x_sample.shape[0]
    assert x_sample.shape[1] == 1
    n_meta = meta_tokens.shape[0]
    depth = norm_mix.shape[0]
    n_a = mu.shape[0]
    HA, NA = r_k.shape[1], r_k.shape[2]
    d_ff = ffn_conv_b.shape[1]
    conv_w = ffn_conv_w.shape[1]
    assert conv_w == 3
    R = g_ckv.shape[0]
    rope = w_dkv.shape[1] - R
    HB = w_ukv.shape[1]
    nope = g_k.shape[0] - rope // 2
    vdim = w_ukv.shape[2] - nope
    qk = nope + rope
    T = seq + n_meta
    Tp = -(-T // ROW_ALIGN) * ROW_ALIGN
    Tpa = -(-Tp // ATTN_ALIGN) * ATTN_ALIGN
    MP = B * Tp
    M = MP + DB
    past_len = page_table.shape[1] * cache_ckv.shape[1]
    scale = qk ** -0.5

    h0 = jnp.concatenate([jnp.broadcast_to(meta_tokens[None], (B, n_meta, D)), x_prompt], axis=1)
    h0 = jnp.pad(h0, ((0, 0), (0, Tp - T), (0, 0)))
    h = jnp.concatenate([h0.reshape(MP, D), x_sample.reshape(DB, D)], axis=0)

    split = lambda x: (x[:MP].reshape(B, Tp, -1), x[MP:])
    join = lambda p, s: jnp.concatenate([p.reshape(MP, -1), s], axis=0)

    pos = jnp.concatenate([jnp.tile(jnp.arange(Tp), B), jnp.full((DB,), past_len)]).astype(F32)
    inv_freq = ROPE_THETA ** (-jnp.arange(0, rope, 2, dtype=F32) / rope)
    ang = pos[:, None] * inv_freq[None]
    cos, sin = jnp.cos(ang), jnp.sin(ang)

    shift_p, shift_s, wkv_p, wkv_s, conv_p, conv_s = [], [], [], [], [], []
    v_first = None
    kv_p = kv_s = None
    ckv = kpe = None

    for i in range(depth):
        if i < n_a:
            xn = _rmsnorm(h, norm_mix[i], F32)
            xn_p, xn_s = split(xn)
            shift_p.append(xn_p[:, T - 1])
            shift_s.append(xn_s)
            prev = join(jnp.concatenate([jnp.zeros((B, 1, D), F32), xn_p[:, :-1]], axis=1), state_shift[i])
            dx = prev - xn
            xs = [(xn + dx * mu[i, s]).astype(BF16) for s in range(6)]
            r = _mm(xs[0], w_rkv[i, 0])
            k = _mm(xs[1], w_rkv[i, 1])
            v = _mm(xs[2], w_rkv[i, 2])
            w_log = -jax.nn.softplus(-(w0[i] + _lora(xs[3], w1[i], w2[i], jnp.tanh))) - 0.5
            log_decay = -jnp.exp(w_log)
            if i > 0:
                v = v + (v_first - v) * jax.nn.sigmoid(v0[i - 1] + _lora(xs[2], v1[i - 1], v2[i - 1], None))
            else:
                v_first = v
            a = jax.nn.sigmoid(a0[i] + _lora(xs[4], a1[i], a2[i], None))
            g = _lora(xs[5], g1[i], g2[i], jax.nn.sigmoid)
            kk = (k * k_k[i]).reshape(M, HA, NA)
            kk = (kk / jnp.maximum(jnp.sqrt(jnp.sum(kk * kk, axis=-1, keepdims=True)), 1e-12)).reshape(M, D)
            k = k * (1.0 + (a - 1.0) * k_a[i])
            sa, sb = -kk, kk * a
            (r_p, r_s), (l_p, l_s), (k_p, k_s), (v_p, v_s), (a_p, a_s), (b_p, b_s) = (
                split(t) for t in (r, log_decay, k, v, sa, sb))
            y_p, st_p = _wkv_chunked(r_p, l_p, k_p, v_p, a_p, b_p, T, NA)
            hd = lambda t: t.reshape(DB, HA, NA)
            y_s, st_s = _wkv_step(hd(r_s), hd(jnp.exp(l_s)), hd(k_s), hd(v_s), hd(a_s), hd(b_s), state_wkv[i])
            wkv_p.append(st_p)
            wkv_s.append(st_s)
            y = join(y_p, y_s.reshape(DB, D)).reshape(M, HA, NA)
            mean = jnp.mean(y, axis=-1, keepdims=True)
            var = jnp.mean(jnp.square(y - mean), axis=-1, keepdims=True)
            yn = ((y - mean) * lax.rsqrt(var + NA * GN_EPS_PER_CHANNEL)).reshape(M, D) * lnx_w[i] + lnx_b[i]
            bonus = jnp.sum((r * k).reshape(M, HA, NA) * r_k[i], axis=-1, keepdims=True) * v.reshape(M, HA, NA)
            h = _mm(((yn + bonus.reshape(M, D)) * g).astype(BF16), w_o_a[i], residual=h)
        else:
            j = i - n_a
            xn = _rmsnorm(h, norm_mix[i], BF16)
            cq = _rmsnorm(_mm(xn, w_dq[j]), g_q[j], BF16)
            q = _mm(cq, w_uq[j].reshape(-1, HB * qk)).reshape(M, HB, qk)
            q = jnp.concatenate([q[..., :nope], _rope(q[..., nope:], cos[:, None], sin[:, None])], axis=-1)
            gq = g_qn[j]
            q = _head_rms(q, jnp.concatenate([gq[:nope], gq[nope:], gq[nope:]])) * scale
            q_p, q_s = q[:MP].reshape(B, Tp, HB, qk), q[MP:]
            q_p = jnp.pad(q_p.transpose(0, 2, 1, 3), ((0, 0), (0, 0), (0, Tpa - Tp), (0, 0))).astype(BF16)
            o_p = _flash(q_p, kv_p[0], kv_p[1], tq=Tpa // 3, tk=Tpa // 3)[:, :Tp]
            gk = g_k
            qn = q_s[..., :nope] * gk[:nope]
            qp = q_s[..., nope:] * jnp.concatenate([gk[nope:], gk[nope:]])
            qa = _bmm(qn.transpose(1, 0, 2), kv_s['wnt3']).transpose(1, 0, 2)
            acc, m, l = _paged_attn(page_table, cache_ckv, cache_kpe, qa, qp, kv_s['wnt'], nope)
            c_new, kp_new = kv_s['ckv'], kv_s['kpe']
            s_new = kv_s['rinv'] * (jnp.sum(qa * c_new[:, None, :], axis=-1) + jnp.sum(qp * kp_new[:, None, :], axis=-1))
            m_f = jnp.maximum(m, s_new)
            alpha = jnp.exp(m - m_f)
            pn = jnp.exp(s_new - m_f)
            l_f = l * alpha + pn
            ctx = (acc * alpha[..., None] + pn[..., None] * c_new[:, None, :]) / l_f[..., None]
            o_s = _bmm(ctx.transpose(1, 0, 2), kv_s['wv']).transpose(1, 0, 2).reshape(DB, HB * vdim)
            attn = jnp.concatenate([o_p.reshape(MP, HB * vdim), o_s.astype(BF16)], axis=0)
            h = _mm(attn, w_o_b[j].reshape(HB * vdim, D), residual=h)

        xn = _rmsnorm(h, norm_ffn[i], BF16)
        u = _mm(xn, ffn_w_in[i])
        c, z = u[:, :d_ff], u[:, d_ff:]
        c_p, c_s = split(c)
        cw, cb = ffn_conv_w[i], ffn_conv_b[i]
        cf = jnp.concatenate([jnp.zeros((B, 2, d_ff), F32), c_p], axis=1)
        conv_pp = cb + cf[:, 0:Tp] * cw[0] + cf[:, 1:Tp + 1] * cw[1] + cf[:, 2:Tp + 2] * cw[2]
        buf = state_conv[i]
        conv_ss = cb + buf[:, 0] * cw[0] + buf[:, 1] * cw[1] + c_s * cw[2]
        conv_p.append(c_p[:, T - 2:T])
        conv_s.append(jnp.stack([buf[:, 1], c_s], axis=1))
        gated = (jax.nn.silu(join(conv_pp, conv_ss)) * z).astype(BF16)
        h = _mm(gated, ffn_w_out[i], residual=h)

        if i == n_a - 1:
            xk = _rmsnorm(h, norm_kv, BF16)
            ckv = _rmsnorm(_mm(xk, w_dkv[:, :R]), g_ckv, F32)
            kpe = _rope(_mm(xk, w_dkv[:, R:]), cos, sin)
            kv = _mm(ckv.astype(BF16), w_ukv.reshape(R, HB * (nope + vdim))).reshape(M, HB, nope + vdim)
            k_nope = kv[..., :nope]
            ssq = jnp.sum(k_nope * k_nope, axis=-1) + jnp.sum(kpe * kpe, axis=-1, keepdims=True)
            rinv = lax.rsqrt(ssq / qk + RMS_EPS)
            gk_full = jnp.concatenate([g_k[:nope], g_k[nope:], g_k[nope:]])
            k_full = jnp.concatenate([k_nope, jnp.broadcast_to(kpe[:, None, :], (M, HB, rope))], axis=-1)
            k_full = k_full * rinv[..., None] * gk_full
            to_heads = lambda t: jnp.pad(t[:MP].reshape(B, Tp, HB, -1).transpose(0, 2, 1, 3),
                                         ((0, 0), (0, 0), (0, Tpa - Tp), (0, 0))).astype(BF16)
            kv_p = (to_heads(k_full), to_heads(kv[..., nope:]))
            w_nope = w_ukv[:, :, :nope]
            wnt3 = w_nope.transpose(1, 2, 0)
            kv_s = dict(ckv=ckv[MP:], kpe=kpe[MP:], rinv=rinv[MP:], wnt3=wnt3,
                        wnt=wnt3.reshape(HB * nope, R).astype(BF16),
                        wv=w_ukv[:, :, nope:].transpose(1, 0, 2))

    h_p, h_s = split(h)
    ckv_p, ckv_s = split(ckv)
    kpe_p, kpe_s = split(kpe)
    return (h_p[:, n_meta:T], h_s.reshape(DB, 1, D),
            jnp.stack(shift_p), jnp.stack(wkv_p), jnp.stack(conv_p), ckv_p[:, :T], kpe_p[:, :T],
            jnp.stack(shift_s), jnp.stack(wkv_s), jnp.stack(conv_s),
            ckv_s.reshape(DB, 1, R), kpe_s.reshape(DB, 1, rope))
```

```python
import functools

import numpy as np
import jax
import jax.numpy as jnp
from jax import lax
from jax.experimental import pallas as pl
from jax.experimental.pallas import tpu as pltpu

F32 = jnp.float32
BF16 = jnp.bfloat16

RMS_EPS = 1e-6
ROPE_THETA = 10000.0
GN_EPS_PER_CHANNEL = 1e-5
LOG2E = 1.4426950408889634
LANES = 128
SUBLANES = 8
MXU_DIM = 256
WKV_CHUNK = 64
WKV_UNITS = 4
ROW_ALIGN = 64
ATTN_ALIGN = 128
FLASH_BLOCKS = 3
PAGES_PER_STEP = 8
VMEM_LIMIT = 56 * 1024 * 1024


def _nt(a, b):
    return lax.dot_general(a, b, (((1,), (1,)), ((), ())), preferred_element_type=F32)


def _tn(a, b):
    return lax.dot_general(a, b, (((0,), (0,)), ((), ())), preferred_element_type=F32)


def _nn(a, b):
    return jnp.dot(a, b, preferred_element_type=F32)


def _pick(n, candidates):
    for c in candidates:
        if n % c == 0:
            return c
    return n


def _mm_kernel(x_ref, w_ref, *rest, act, has_res):
    if has_res:
        r_ref, o_ref, wb_ref = rest
    else:
        o_ref, wb_ref = rest

    @pl.when(pl.program_id(1) == 0)
    def _():
        wb_ref[...] = w_ref[...].astype(BF16)

    acc = _nn(x_ref[...].astype(BF16), wb_ref[...])
    if act is not None:
        acc = act(acc)
    if has_res:
        acc = acc + r_ref[...]
    o_ref[...] = acc.astype(o_ref.dtype)


def _mm(x, w, *, act=None, residual=None, out_dtype=F32):
    M, K = x.shape
    N = w.shape[1]
    tm = _pick(M, (768, 512, 384, 256, 128))
    if K > 4096:
        tm = _pick(M, (384, 256, 128))
    tn = _pick(N, (1024, 512, 256, 128)) if K <= 2048 else _pick(N, (512, 256, 128))
    grid = (N // tn, M // tm)
    in_specs = [pl.BlockSpec((tm, K), lambda j, i: (i, 0)),
                pl.BlockSpec((K, tn), lambda j, i: (0, j))]
    args = [x, w]
    if residual is not None:
        in_specs.append(pl.BlockSpec((tm, tn), lambda j, i: (i, j)))
        args.append(residual)
    return pl.pallas_call(
        functools.partial(_mm_kernel, act=act, has_res=residual is not None),
        grid=grid,
        in_specs=in_specs,
        out_specs=pl.BlockSpec((tm, tn), lambda j, i: (i, j)),
        out_shape=jax.ShapeDtypeStruct((M, N), out_dtype),
        scratch_shapes=[pltpu.VMEM((K, tn), BF16)],
        compiler_params=pltpu.CompilerParams(
            dimension_semantics=("arbitrary", "arbitrary"), vmem_limit_bytes=VMEM_LIMIT),
    )(*args)


def _bmm_kernel(x_ref, w_ref, o_ref):
    o_ref[0] = _nn(x_ref[0].astype(BF16), w_ref[0].astype(BF16))


def _bmm(x, w):
    G, M, K = x.shape
    N = w.shape[2]
    return pl.pallas_call(
        _bmm_kernel,
        grid=(G,),
        in_specs=[pl.BlockSpec((1, M, K), lambda g: (g, 0, 0)),
                  pl.BlockSpec((1, K, N), lambda g: (g, 0, 0))],
        out_specs=pl.BlockSpec((1, M, N), lambda g: (g, 0, 0)),
        out_shape=jax.ShapeDtypeStruct((G, M, N), F32),
    )(x, w)


def _rms_kernel(x_ref, g_ref, o_ref):
    x = x_ref[...].astype(F32)
    y = x * lax.rsqrt(jnp.mean(x * x, axis=-1, keepdims=True) + RMS_EPS)
    o_ref[...] = (y * g_ref[...]).astype(o_ref.dtype)


def _rmsnorm(x, g, out_dtype):
    M, D = x.shape
    tm = _pick(M, (768, 512, 384, 256, 128))
    return pl.pallas_call(
        _rms_kernel,
        grid=(M // tm,),
        in_specs=[pl.BlockSpec((tm, D), lambda i: (i, 0)),
                  pl.BlockSpec((1, D), lambda i: (0, 0))],
        out_specs=pl.BlockSpec((tm, D), lambda i: (i, 0)),
        out_shape=jax.ShapeDtypeStruct((M, D), out_dtype),
    )(x, g.reshape(1, D).astype(F32))


def _gate(c, p1, p2, z, cw_ref, cb_ref):
    conv = cb_ref[...] + p2 * cw_ref[0:1, :] + p1 * cw_ref[1:2, :] + c * cw_ref[2:3, :]
    return (jax.nn.silu(conv) * z).astype(BF16)


def _ffn_in_prompt_kernel(x_ref, wc_ref, wz_ref, cw_ref, cb_ref, g_ref, tail_ref, wcb_ref, wzb_ref, cs_ref,
                          *, tm, seq_rows, n_seq, tails):
    i = pl.program_id(1)
    S = SUBLANES

    @pl.when(i == 0)
    def _():
        wcb_ref[...] = wc_ref[...].astype(BF16)
        wzb_ref[...] = wz_ref[...].astype(BF16)
        cs_ref[0:S, :] = jnp.zeros((S, cs_ref.shape[1]), F32)

    x = x_ref[...]
    c = _nn(x, wcb_ref[...])
    z = _nn(x, wzb_ref[...])
    cs_ref[S:S + tm, :] = c
    p1 = cs_ref[S - 1:S - 1 + tm, :]
    p2 = cs_ref[S - 2:S - 2 + tm, :]
    row = i * tm + lax.broadcasted_iota(jnp.int32, (tm, 1), 0)
    t = row
    for b in range(1, n_seq):
        t = jnp.where(row >= b * seq_rows, row - b * seq_rows, t)
    p1 = jnp.where(t >= 1, p1, 0.0)
    p2 = jnp.where(t >= 2, p2, 0.0)
    g_ref[...] = _gate(c, p1, p2, z, cw_ref, cb_ref)
    cs_ref[0:S, :] = cs_ref[tm:tm + S, :]
    for b, (tile, off) in enumerate(tails):
        @pl.when(i == tile)
        def _(b=b, off=off):
            tail_ref[b] = c[off:off + S]


def _ffn_in_sample_kernel(x_ref, wc_ref, wz_ref, cw_ref, cb_ref, b0_ref, b1_ref, alias_ref, g_ref, c_ref):
    del alias_ref
    x = x_ref[...]
    c = _nn(x, wc_ref[...].astype(BF16))
    z = _nn(x, wz_ref[...].astype(BF16))
    c_ref[...] = c
    g_ref[...] = _gate(c, b1_ref[...], b0_ref[...], z, cw_ref, cb_ref)


def _ffn_in(xn, w_in, conv_w, conv_b, buf, n_seq, seq_rows, t_valid):
    M, K = xn.shape
    F = conv_b.shape[0]
    MP = n_seq * seq_rows
    DB = M - MP
    S = SUBLANES
    tn = _pick(F, (512, 256, 128))
    nj = F // tn
    tm = _pick(MP, (640, 512, 384, 256, 128))
    cw = conv_w.astype(F32)
    cb = conv_b.reshape(1, F).astype(F32)
    t0 = t_valid - 2
    assert t0 % S <= S - 2 and MP % DB == 0
    tails = []
    for b in range(n_seq):
        r0 = b * seq_rows + (t0 // S) * S
        assert r0 // tm == (r0 + S - 1) // tm
        tails.append((r0 // tm, r0 % tm))
    wspec = lambda off: pl.BlockSpec((K, tn), lambda j, i: (0, j + off))
    cspec = lambda rows: pl.BlockSpec((rows, tn), lambda j, i: (0, j))
    gated, tail = pl.pallas_call(
        functools.partial(_ffn_in_prompt_kernel, tm=tm, seq_rows=seq_rows, n_seq=n_seq, tails=tuple(tails)),
        grid=(nj, MP // tm),
        in_specs=[pl.BlockSpec((tm, K), lambda j, i: (i, 0)), wspec(0), wspec(nj), cspec(3), cspec(1)],
        out_specs=[pl.BlockSpec((tm, tn), lambda j, i: (i, j)),
                   pl.BlockSpec((n_seq, S, tn), lambda j, i: (0, 0, j))],
        out_shape=[jax.ShapeDtypeStruct((M, F), BF16), jax.ShapeDtypeStruct((n_seq, S, F), F32)],
        scratch_shapes=[pltpu.VMEM((K, tn), BF16), pltpu.VMEM((K, tn), BF16), pltpu.VMEM((tm + S, tn), F32)],
        compiler_params=pltpu.CompilerParams(
            dimension_semantics=("arbitrary", "arbitrary"), vmem_limit_bytes=VMEM_LIMIT),
    )(xn, w_in, w_in, cw, cb)
    wspec1 = lambda off: pl.BlockSpec((K, tn), lambda j: (0, j + off))
    cspec1 = lambda rows: pl.BlockSpec((rows, tn), lambda j: (0, j))
    gated, c_s = pl.pallas_call(
        _ffn_in_sample_kernel,
        grid=(nj,),
        in_specs=[pl.BlockSpec((DB, K), lambda j: (MP // DB, 0)), wspec1(0), wspec1(nj), cspec1(3), cspec1(1),
                  cspec1(DB), cspec1(DB), pl.BlockSpec(memory_space=pl.ANY)],
        out_specs=[pl.BlockSpec((DB, tn), lambda j: (MP // DB, j)), cspec1(DB)],
        out_shape=[jax.ShapeDtypeStruct((M, F), BF16), jax.ShapeDtypeStruct((DB, F), F32)],
        input_output_aliases={7: 0},
        compiler_params=pltpu.CompilerParams(dimension_semantics=("arbitrary",), vmem_limit_bytes=VMEM_LIMIT),
    )(xn, w_in, w_in, cw, cb, buf[:, 0], buf[:, 1], gated)
    off = t0 % S
    return gated, tail[:, off:off + 2], c_s


def _wkv_unit(r, lw, k, v, a, b, st, consts, *, chunk, groups, levels):
    bd, ts, ti, tri, eye = consts
    hi = lw.astype(BF16)
    rem = lw - hi.astype(F32)
    mid = rem.astype(BF16)
    lo = (rem - mid.astype(F32)).astype(BF16)
    lc = _nn(tri, hi) + _nn(tri, mid) + _nn(tri, lo)
    lc_end = lc[chunk - 1:chunk, :]
    e_neg = jnp.exp(-lc)
    e_end = jnp.exp(lc_end - lc)
    rt = r * jnp.exp(lc)
    at = a * jnp.exp(lc - lw)
    kt = k * e_neg
    bt = b * e_neg
    kh = k * e_end
    bh = b * e_end
    d_end = jnp.exp(lc_end)

    def stack(x):
        return jnp.concatenate([x] * groups, axis=0) * bd

    rt_f = stack(rt)
    at_s, kt_s, bt_s, kh_s, bh_s, v_s = (stack(x).astype(BF16) for x in (at, kt, bt, kh, bh, v))
    rt_s = rt_f.astype(BF16)

    m_ab = jnp.where(ts, _nt(at_s, bt_s), 0.0)
    m_ak = jnp.where(ts, _nt(at_s, kt_s), 0.0)
    n_rb = jnp.where(ti, _nt(rt_s, bt_s), 0.0).astype(BF16)
    n_rk = jnp.where(ti, _nt(rt_s, kt_s), 0.0).astype(BF16)

    inv = eye + m_ab
    pw = m_ab
    for _ in range(levels):
        pwb = pw.astype(BF16)
        pw = _nn(pwb, pwb)
        inv = inv + _nn(inv.astype(BF16), pw.astype(BF16))
    inv_b = inv.astype(BF16)

    p_b = _nn(inv_b, at_s).astype(BF16)
    q_b = _nn(inv_b, _nn(m_ak.astype(BF16), v_s).astype(BF16)).astype(BF16)

    g_mat = eye * d_end + _tn(bh_s, p_b)
    f_mat = _tn(bh_s, q_b) + _tn(kh_s, v_s)
    r_y = rt_f + _nn(n_rb, p_b)
    y_0 = _nn(n_rb, q_b) + _nn(n_rk, v_s)

    st_b = st.astype(BF16)
    y_s = _nn(r_y.astype(BF16), st_b) + y_0
    y = y_s[0:chunk]
    for h in range(1, groups):
        y = y + y_s[h * chunk:(h + 1) * chunk]
    return y, _nn(g_mat.astype(BF16), st_b) + f_mat


def _wkv_chunk_kernel(r_ref, lw_ref, k_ref, v_ref, a_ref, b_ref,
                      bd_ref, ts_ref, ti_ref, tri_ref, eye_ref,
                      y_ref, s_ref, st_ref, *, chunk, groups, units, t_valid, n_chunks, levels):
    c = pl.program_id(1)
    W = MXU_DIM

    @pl.when(c == 0)
    def _():
        st_ref[...] = jnp.zeros_like(st_ref)

    row = c * chunk + lax.broadcasted_iota(jnp.int32, (chunk, 1), 0)
    valid = row < t_valid
    consts = (bd_ref[...], ts_ref[...] > 0.0, ti_ref[...] > 0.0, tri_ref[...], eye_ref[...])
    for u in range(units):
        lanes = slice(u * W, (u + 1) * W)
        ins = (jnp.where(valid, ref[:, lanes], 0.0) for ref in (r_ref, lw_ref, k_ref, v_ref, a_ref, b_ref))
        y, st = _wkv_unit(*ins, st_ref[u], consts, chunk=chunk, groups=groups, levels=levels)
        y_ref[:, lanes] = y
        st_ref[u] = st

    @pl.when(c == n_chunks - 1)
    def _():
        s_ref[0] = st_ref[...]


def _wkv_chunked(r, lw, k, v, a, b, n_seq, seq_rows, t_valid, head):
    M, D = r.shape
    L = WKV_CHUNK
    W = MXU_DIM
    U = WKV_UNITS
    G = W // head
    nu = D // (W * U)
    nc = seq_rows // L
    assert seq_rows % L == 0 and D % (W * U) == 0 and G * L == W
    idx = np.arange(W)
    same = (idx[:, None] // L) == (idx[None, :] // L)
    bd = same.astype(np.float32)
    ts = (same & (idx[None, :] < idx[:, None])).astype(np.float32)
    ti = (same & (idx[None, :] <= idx[:, None])).astype(np.float32)
    tri = jnp.asarray(np.tril(np.ones((L, L), np.float32)), BF16)
    eye = np.eye(W, dtype=np.float32)
    levels = int(np.log2(L)) - 1
    assert 2 ** (levels + 1) == L

    seq = pl.BlockSpec((L, W * U), lambda p, c: ((p // nu) * nc + c, p % nu))
    const = lambda shape: pl.BlockSpec(shape, lambda p, c: (0, 0))
    y, st = pl.pallas_call(
        functools.partial(_wkv_chunk_kernel, chunk=L, groups=G, units=U, t_valid=t_valid, n_chunks=nc,
                          levels=levels),
        grid=(n_seq * nu, nc),
        in_specs=[seq] * 6 + [const((W, W))] * 3 + [const((L, L)), const((W, W))],
        out_specs=[seq, pl.BlockSpec((1, U, W, W), lambda p, c: (p, 0, 0, 0))],
        out_shape=[jax.ShapeDtypeStruct((M, D), F32),
                   jax.ShapeDtypeStruct((n_seq * nu, U, W, W), F32)],
        scratch_shapes=[pltpu.VMEM((U, W, W), F32)],
        compiler_params=pltpu.CompilerParams(dimension_semantics=("arbitrary", "arbitrary")),
    )(r, lw, k, v, a, b, jnp.asarray(bd), jnp.asarray(ts), jnp.asarray(ti), tri, jnp.asarray(eye))
    ng = nu * U
    st = st.reshape(n_seq, ng, G, head, G, head)
    st = jnp.stack([st[:, :, h, :, h, :] for h in range(G)], axis=2)
    return y, jnp.swapaxes(st, -1, -2).reshape(n_seq, ng * G, head, head)


def _wkv_step_kernel(r_ref, lw_ref, k_ref, v_ref, a_ref, b_ref, s_ref, y_ref, so_ref, *, rows, heads, head):
    ii = lax.broadcasted_iota(jnp.int32, (head, head), 0)
    jj = lax.broadcasted_iota(jnp.int32, (head, head), 1)
    eye = ii == jj

    def one_row(n, carry):
        for h in range(heads):
            vec = lambda ref: ref[n, h:h + 1, :]
            s = s_ref[n, h]
            sa = jnp.sum(s * vec(a_ref), axis=-1, keepdims=True)
            v_col = jnp.sum(jnp.where(eye, vec(v_ref), 0.0), axis=-1, keepdims=True)
            s_new = s * jnp.exp(vec(lw_ref)) + sa * vec(b_ref) + v_col * vec(k_ref)
            so_ref[n, h] = s_new
            y_col = jnp.sum(s_new * vec(r_ref), axis=-1, keepdims=True)
            y_ref[n, h:h + 1, :] = jnp.sum(jnp.where(eye, y_col, 0.0), axis=0, keepdims=True)
        return carry

    lax.fori_loop(0, rows, one_row, 0, unroll=2)


def _wkv_step(r, lw, k, v, a, b, s0):
    DB, H, N, _ = s0.shape
    rows = _pick(DB, (8, 4, 2))
    vec = pl.BlockSpec((rows, H, N), lambda i: (i, 0, 0))
    mat = pl.BlockSpec((rows, H, N, N), lambda i: (i, 0, 0, 0))
    return pl.pallas_call(
        functools.partial(_wkv_step_kernel, rows=rows, heads=H, head=N),
        grid=(DB // rows,),
        in_specs=[vec] * 6 + [mat],
        out_specs=[vec, mat],
        out_shape=[jax.ShapeDtypeStruct((DB, H, N), F32), jax.ShapeDtypeStruct((DB, H, N, N), F32)],
        compiler_params=pltpu.CompilerParams(dimension_semantics=("arbitrary",), vmem_limit_bytes=VMEM_LIMIT),
    )(r, lw, k, v, a, b, s0)


def _flash_kernel(qi_ref, ki_ref, flag_ref, q_ref, k_ref, v_ref, o_ref, m_sc, acc_sc, *, tq, tk, vdim):
    p = pl.program_id(2)
    qi = qi_ref[p]
    ki = ki_ref[p]
    flags = flag_ref[p]

    @pl.when(ki == 0)
    def _():
        m_sc[...] = jnp.full_like(m_sc, -jnp.inf)
        acc_sc[...] = jnp.zeros_like(acc_sc)

    def update(masked):
        s = _nt(q_ref[0, 0], k_ref[0, 0])
        if masked:
            qpos = qi * tq + lax.broadcasted_iota(jnp.int32, (tq, 1), 0)
            kpos = ki * tk + lax.broadcasted_iota(jnp.int32, (1, tk), 1)
            s = jnp.where(kpos <= qpos, s, -jnp.inf)
        m_old = m_sc[...]
        m_new = jnp.maximum(m_old, jnp.max(s, axis=-1, keepdims=True))
        pm = jnp.exp2(s - m_new).astype(BF16)
        acc_sc[...] = jnp.exp2(m_old - m_new) * acc_sc[...] + _nn(pm, v_ref[0, 0])
        m_sc[...] = m_new

    pl.when((flags & 2) != 0)(lambda: update(True))
    pl.when((flags & 2) == 0)(lambda: update(False))

    @pl.when((flags & 1) != 0)
    def _():
        acc = acc_sc[...]
        o_ref[0] = (acc[:, :vdim] / acc[:, vdim:vdim + 1]).astype(o_ref.dtype)


def _flash(q, k, v, vdim, tq, tk):
    B, H, T, E = q.shape
    VA = v.shape[-1]
    pairs = [(qi, ki) for qi in range(T // tq) for ki in range(T // tk) if ki * tk <= qi * tq + tq - 1]
    n = len(pairs)
    qi_tab = np.array([p[0] for p in pairs], np.int32)
    ki_tab = np.array([p[1] for p in pairs], np.int32)
    flags = np.array([(1 if (i + 1 == n or pairs[i + 1][0] != pairs[i][0]) else 0)
                      + (2 if (ki + 1) * tk - 1 > qi * tq else 0)
                      for i, (qi, ki) in enumerate(pairs)], np.int32)
    grid_spec = pltpu.PrefetchScalarGridSpec(
        num_scalar_prefetch=3,
        grid=(B, H, n),
        in_specs=[pl.BlockSpec((1, 1, tq, E), lambda b, h, p, qt, kt, ft: (b, h, qt[p], 0)),
                  pl.BlockSpec((1, 1, tk, E), lambda b, h, p, qt, kt, ft: (b, h, kt[p], 0)),
                  pl.BlockSpec((1, 1, tk, VA), lambda b, h, p, qt, kt, ft: (b, h, kt[p], 0))],
        out_specs=pl.BlockSpec((1, tq, vdim), lambda b, h, p, qt, kt, ft: (b, qt[p], h)),
        scratch_shapes=[pltpu.VMEM((tq, 1), F32), pltpu.VMEM((tq, VA), F32)],
    )
    return pl.pallas_call(
        functools.partial(_flash_kernel, tq=tq, tk=tk, vdim=vdim),
        grid_spec=grid_spec,
        out_shape=jax.ShapeDtypeStruct((B, T, H * vdim), BF16),
        compiler_params=pltpu.CompilerParams(
            dimension_semantics=("arbitrary", "arbitrary", "arbitrary"), vmem_limit_bytes=VMEM_LIMIT),
    )(jnp.asarray(qi_tab), jnp.asarray(ki_tab), jnp.asarray(flags), q, k, v)


def _paged_kernel(pt_ref, *refs, heads, nope, qk_dim, n_steps, pages, have_rinv):
    del pt_ref
    c_refs, p_refs = refs[:pages], refs[pages:2 * pages]
    qa_ref, qp_ref, x_ref = refs[2 * pages:2 * pages + 3]
    rest = refs[2 * pages + 3:]
    if have_rinv:
        acc_ref, m_ref, l_ref, m_sc, l_sc, acc_sc = rest
    else:
        acc_ref, m_ref, l_ref, rinv_ref, m_sc, l_sc, acc_sc = rest
    step = pl.program_id(1)

    @pl.when(step == 0)
    def _():
        m_sc[...] = jnp.full_like(m_sc, -jnp.inf)
        l_sc[...] = jnp.zeros_like(l_sc)
        acc_sc[...] = jnp.zeros_like(acc_sc)

    c = jnp.concatenate([r[0] for r in c_refs], axis=0).astype(BF16)
    kp = jnp.concatenate([r[0] for r in p_refs], axis=0)
    tokens = c.shape[0]
    if have_rinv:
        rinv = x_ref[0, 0]
    else:
        sub = MXU_DIM
        parts = []
        for t in range(0, tokens, sub):
            knt = _nt(x_ref[...], c[t:t + sub])
            parts.append(jnp.sum((knt * knt).reshape(heads, nope, sub), axis=1))
        ssq = jnp.concatenate(parts, axis=1)
        ones = jnp.ones((SUBLANES, kp.shape[1]), BF16)
        kss = _nt(ones, (kp * kp).astype(BF16))[0:1, :]
        rinv = lax.rsqrt((ssq + kss) * (1.0 / qk_dim) + RMS_EPS)
        rinv_ref[0, 0] = rinv
    s = (_nt(qa_ref[0].astype(BF16), c) + _nt(qp_ref[0].astype(BF16), kp.astype(BF16))) * rinv
    m_old = m_sc[...]
    m_new = jnp.maximum(m_old, jnp.max(s, axis=-1, keepdims=True))
    alpha = jnp.exp(m_old - m_new)
    pm = jnp.exp(s - m_new)
    l_sc[...] = alpha * l_sc[...] + jnp.sum(pm, axis=-1, keepdims=True)
    acc_sc[...] = alpha * acc_sc[...] + _nn(pm.astype(BF16), c)
    m_sc[...] = m_new

    @pl.when(step == n_steps - 1)
    def _():
        acc_ref[0] = acc_sc[...]
        m_ref[0] = jnp.broadcast_to(m_sc[...], m_ref.shape[1:])
        l_ref[0] = jnp.broadcast_to(l_sc[...], l_ref.shape[1:])


def _paged_attn(page_table, cache_ckv, cache_kpe, qa, qp, nope, *, wnt=None, rinv=None):
    DB, n_pages = page_table.shape
    _, page, R = cache_ckv.shape
    rope = cache_kpe.shape[-1]
    H = qa.shape[1]
    P = PAGES_PER_STEP
    assert n_pages % P == 0
    n_steps = n_pages // P
    have_rinv = rinv is not None
    page_spec = lambda width, i: pl.BlockSpec((1, page, width), lambda b, s, pt: (pt[b, P * s + i], 0, 0))
    rinv_spec = pl.BlockSpec((1, 1, H, P * page), lambda b, s, pt: (b, s, 0, 0))
    x_spec = rinv_spec if have_rinv else pl.BlockSpec(wnt.shape, lambda b, s, pt: (0, 0))
    stat_spec = pl.BlockSpec((1, H, LANES), lambda b, s, pt: (b, 0, 0))
    out_specs = [pl.BlockSpec((1, H, R), lambda b, s, pt: (b, 0, 0)), stat_spec, stat_spec]
    out_shape = [jax.ShapeDtypeStruct((DB, H, R), F32),
                 jax.ShapeDtypeStruct((DB, H, LANES), F32),
                 jax.ShapeDtypeStruct((DB, H, LANES), F32)]
    if not have_rinv:
        out_specs.append(rinv_spec)
        out_shape.append(jax.ShapeDtypeStruct((DB, n_steps, H, P * page), F32))
    grid_spec = pltpu.PrefetchScalarGridSpec(
        num_scalar_prefetch=1,
        grid=(DB, n_steps),
        in_specs=([page_spec(R, i) for i in range(P)] + [page_spec(rope, i) for i in range(P)]
                  + [pl.BlockSpec((1, H, R), lambda b, s, pt: (b, 0, 0)),
                     pl.BlockSpec((1, H, rope), lambda b, s, pt: (b, 0, 0)), x_spec]),
        out_specs=out_specs,
        scratch_shapes=[pltpu.VMEM((H, 1), F32), pltpu.VMEM((H, 1), F32), pltpu.VMEM((H, R), F32)],
    )
    outs = pl.pallas_call(
        functools.partial(_paged_kernel, heads=H, nope=nope, qk_dim=nope + rope, n_steps=n_steps, pages=P,
                          have_rinv=have_rinv),
        grid_spec=grid_spec,
        out_shape=out_shape,
        compiler_params=pltpu.CompilerParams(
            dimension_semantics=("arbitrary", "arbitrary"), vmem_limit_bytes=VMEM_LIMIT),
    )(page_table, *([cache_ckv] * P), *([cache_kpe] * P), qa, qp, rinv if have_rinv else wnt)
    acc, m, l = outs[:3]
    return acc, m[:, :, 0], l[:, :, 0], (rinv if have_rinv else outs[3])


def _pad_cols(w, n):
    return jnp.pad(w, ((0, 0), (0, n - w.shape[1])))


def _pad_rows(w, n):
    return jnp.pad(w, ((0, n - w.shape[0]), (0, 0)))


def _lora(x, w1, w2, act):
    rank = w1.shape[1]
    rp = -(-rank // LANES) * LANES
    mid = _mm(x, _pad_cols(w1, rp), act=act, out_dtype=BF16)
    return _mm(mid, _pad_rows(w2, rp))


def _rope(x, cos, sin):
    half = x.shape[-1] // 2
    x1, x2 = x[..., :half], x[..., half:]
    return jnp.concatenate([x1 * cos - x2 * sin, x2 * cos + x1 * sin], axis=-1)


def _head_rms(x, g):
    return x * lax.rsqrt(jnp.mean(x * x, axis=-1, keepdims=True) + RMS_EPS) * g


def kernel(x_prompt, x_sample, state_shift, state_wkv, state_conv, cache_ckv, cache_kpe, page_table, meta_tokens, norm_mix, norm_ffn, mu, w_rkv, w_o_a, w0, w1, w2, a0, a1, a2, v0, v1, v2, g1, g2, k_k, k_a, r_k, lnx_w, lnx_b, ffn_w_in, ffn_conv_w, ffn_conv_b, ffn_w_out, norm_kv, w_dkv, g_ckv, w_ukv, g_k, w_dq, g_q, w_uq, g_qn, w_o_b):
    B, seq, D = x_prompt.shape
    DB = x_sample.shape[0]
    assert x_sample.shape[1] == 1
    n_meta = meta_tokens.shape[0]
    depth = norm_mix.shape[0]
    n_a = mu.shape[0]
    HA, NA = r_k.shape[1], r_k.shape[2]
    assert ffn_conv_w.shape[1] == 3
    R = g_ckv.shape[0]
    rope = w_dkv.shape[1] - R
    HB = w_ukv.shape[1]
    nope = g_k.shape[0] - rope // 2
    vdim = w_ukv.shape[2] - nope
    qk = nope + rope
    T = seq + n_meta
    Tp = -(-T // ROW_ALIGN) * ROW_ALIGN
    Tpa = -(-Tp // ATTN_ALIGN) * ATTN_ALIGN
    MP = B * Tp
    M = MP + DB
    past_len = page_table.shape[1] * cache_ckv.shape[1]
    scale = qk ** -0.5

    h0 = jnp.concatenate([jnp.broadcast_to(meta_tokens[None], (B, n_meta, D)), x_prompt], axis=1)
    h0 = jnp.pad(h0, ((0, 0), (0, Tp - T), (0, 0)))
    h = jnp.concatenate([h0.reshape(MP, D), x_sample.reshape(DB, D)], axis=0)

    t_of_row = jnp.concatenate([jnp.tile(jnp.arange(Tp), B), jnp.full((DB,), past_len)])
    not_first = (t_of_row[:MP] > 0).astype(F32)[:, None]
    inv_freq = ROPE_THETA ** (-jnp.arange(0, rope, 2, dtype=F32) / rope)
    ang = t_of_row.astype(F32)[:, None] * inv_freq[None]
    cos, sin = jnp.cos(ang), jnp.sin(ang)
    last_rows = np.array([b * Tp + T - 1 for b in range(B)])

    shift_p, shift_s, wkv_p, wkv_s, conv_p, conv_s = [], [], [], [], [], []
    v_first = None
    kv_p = kv_s = None
    ckv = kpe = None

    for i in range(depth):
        if i < n_a:
            xn = _rmsnorm(h, norm_mix[i], F32)
            shift_p.append(xn[last_rows])
            shift_s.append(xn[MP:])
            prev = jnp.concatenate([jnp.roll(xn[:MP], 1, axis=0) * not_first, state_shift[i]], axis=0)
            dx = prev - xn
            xs = [(xn + dx * mu[i, s]).astype(BF16) for s in range(6)]
            r = _mm(xs[0], w_rkv[i, 0])
            k = _mm(xs[1], w_rkv[i, 1])
            v = _mm(xs[2], w_rkv[i, 2])
            w_log = -jax.nn.softplus(-(w0[i] + _lora(xs[3], w1[i], w2[i], jnp.tanh))) - 0.5
            log_decay = -jnp.exp(w_log)
            if i > 0:
                v = v + (v_first - v) * jax.nn.sigmoid(v0[i - 1] + _lora(xs[2], v1[i - 1], v2[i - 1], None))
            else:
                v_first = v
            a = jax.nn.sigmoid(a0[i] + _lora(xs[4], a1[i], a2[i], None))
            g = _lora(xs[5], g1[i], g2[i], jax.nn.sigmoid)
            kk = (k * k_k[i]).reshape(M, HA, NA)
            kk = (kk / jnp.maximum(jnp.sqrt(jnp.sum(kk * kk, axis=-1, keepdims=True)), 1e-12)).reshape(M, D)
            k = k * (1.0 + (a - 1.0) * k_a[i])
            sa, sb = -kk, kk * a
            y, st_p = _wkv_chunked(r, log_decay, k, v, sa, sb, B, Tp, T, NA)
            y_s, st_s = _wkv_step(*(t[MP:].reshape(DB, HA, NA) for t in (r, log_decay, k, v, sa, sb)), state_wkv[i])
            y = lax.dynamic_update_slice(y, y_s.reshape(DB, D), (MP, 0))
            wkv_p.append(st_p)
            wkv_s.append(st_s)
            y = y.reshape(M, HA, NA)
            mean = jnp.mean(y, axis=-1, keepdims=True)
            var = jnp.mean(jnp.square(y - mean), axis=-1, keepdims=True)
            yn = ((y - mean) * lax.rsqrt(var + NA * GN_EPS_PER_CHANNEL)).reshape(M, D) * lnx_w[i] + lnx_b[i]
            bonus = jnp.sum((r * k).reshape(M, HA, NA) * r_k[i], axis=-1, keepdims=True) * v.reshape(M, HA, NA)
            h = _mm(((yn + bonus.reshape(M, D)) * g).astype(BF16), w_o_a[i], residual=h)
        else:
            j = i - n_a
            xn = _rmsnorm(h, norm_mix[i], BF16)
            cq = _rmsnorm(_mm(xn, w_dq[j]), g_q[j], BF16)
            q = _mm(cq, w_uq[j].reshape(-1, HB * qk)).reshape(M, HB, qk)
            q = jnp.concatenate([q[..., :nope], _rope(q[..., nope:], cos[:, None], sin[:, None])], axis=-1)
            gq = g_qn[j]
            q = _head_rms(q, jnp.concatenate([gq[:nope], gq[nope:], gq[nope:]])) * scale
            q_p, q_s = (q[:MP] * LOG2E).reshape(B, Tp, HB, qk), q[MP:]
            q_p = jnp.pad(q_p.transpose(0, 2, 1, 3), ((0, 0), (0, 0), (0, Tpa - Tp), (0, 0))).astype(BF16)
            o_p = _flash(q_p, kv_p[0], kv_p[1], vdim, tq=Tpa // FLASH_BLOCKS, tk=Tpa // FLASH_BLOCKS)[:, :Tp]
            qn = q_s[..., :nope] * g_k[:nope]
            qp = q_s[..., nope:] * jnp.concatenate([g_k[nope:], g_k[nope:]])
            qa = _bmm(qn.transpose(1, 0, 2), kv_s['wnt3']).transpose(1, 0, 2)
            acc, m, l, kv_s['rinv_cache'] = _paged_attn(
                page_table, cache_ckv, cache_kpe, qa, qp, nope, wnt=kv_s['wnt'], rinv=kv_s['rinv_cache'])
            c_new, kp_new = kv_s['ckv'], kv_s['kpe']
            s_new = kv_s['rinv'] * (jnp.sum(qa * c_new[:, None, :], axis=-1) + jnp.sum(qp * kp_new[:, None, :], axis=-1))
            m_f = jnp.maximum(m, s_new)
            alpha = jnp.exp(m - m_f)
            pn = jnp.exp(s_new - m_f)
            l_f = l * alpha + pn
            ctx = (acc * alpha[..., None] + pn[..., None] * c_new[:, None, :]) / l_f[..., None]
            o_s = _bmm(ctx.transpose(1, 0, 2), kv_s['wv']).transpose(1, 0, 2).reshape(DB, HB * vdim)
            attn = jnp.concatenate([o_p.reshape(MP, HB * vdim), o_s.astype(BF16)], axis=0)
            h = _mm(attn, w_o_b[j].reshape(HB * vdim, D), residual=h)

        xn = _rmsnorm(h, norm_ffn[i], BF16)
        gated, c_tail, c_s = _ffn_in(xn, ffn_w_in[i], ffn_conv_w[i], ffn_conv_b[i], state_conv[i], B, Tp, T)
        conv_p.append(c_tail)
        conv_s.append(jnp.stack([state_conv[i][:, 1], c_s], axis=1))
        h = _mm(gated, ffn_w_out[i], residual=h)

        if i == n_a - 1:
            xk = _rmsnorm(h, norm_kv, BF16)
            ckv = _rmsnorm(_mm(xk, w_dkv[:, :R]), g_ckv, F32)
            kpe = _rope(_mm(xk, w_dkv[:, R:]), cos, sin)
            kv = _mm(ckv.astype(BF16), w_ukv.reshape(R, HB * (nope + vdim))).reshape(M, HB, nope + vdim)
            k_nope = kv[..., :nope]
            ssq = jnp.sum(k_nope * k_nope, axis=-1) + jnp.sum(kpe * kpe, axis=-1, keepdims=True)
            rinv = lax.rsqrt(ssq / qk + RMS_EPS)
            gk_full = jnp.concatenate([g_k[:nope], g_k[nope:], g_k[nope:]])
            k_full = jnp.concatenate([k_nope, jnp.broadcast_to(kpe[:, None, :], (M, HB, rope))], axis=-1)
            k_full = k_full * rinv[..., None] * gk_full
            v_aug = jnp.concatenate([kv[..., nope:], jnp.ones((M, HB, 1), F32), jnp.zeros((M, HB, vdim - 1), F32)],
                                    axis=-1)
            to_heads = lambda t: jnp.pad(t[:MP].reshape(B, Tp, HB, -1).transpose(0, 2, 1, 3),
                                         ((0, 0), (0, 0), (0, Tpa - Tp), (0, 0))).astype(BF16)
            kv_p = (to_heads(k_full), to_heads(v_aug))
            wnt3 = w_ukv[:, :, :nope].transpose(1, 2, 0)
            kv_s = dict(ckv=ckv[MP:], kpe=kpe[MP:], rinv=rinv[MP:], wnt3=wnt3, rinv_cache=None,
                        wnt=wnt3.reshape(HB * nope, R).astype(BF16),
                        wv=w_ukv[:, :, nope:].transpose(1, 0, 2))

    return (h[:MP].reshape(B, Tp, D)[:, n_meta:T], h[MP:].reshape(DB, 1, D),
            jnp.stack(shift_p), jnp.stack(wkv_p), jnp.stack(conv_p),
            ckv[:MP].reshape(B, Tp, R)[:, :T], kpe[:MP].reshape(B, Tp, rope)[:, :T],
            jnp.stack(shift_s), jnp.stack(wkv_s), jnp.stack(conv_s),
            ckv[MP:].reshape(DB, 1, R), kpe[MP:].reshape(DB, 1, rope))
```

```python
import functools

import numpy as np
import jax
import jax.numpy as jnp
from jax import lax
from jax.experimental import pallas as pl
from jax.experimental.pallas import tpu as pltpu

F32 = jnp.float32
BF16 = jnp.bfloat16

RMS_EPS = 1e-6
ROPE_THETA = 10000.0
GN_EPS_PER_CHANNEL = 1e-5
LOG2E = 1.4426950408889634
LANES = 128
SUBLANES = 8
MXU_DIM = 256
WKV_CHUNK = 64
WKV_UNITS = 4
WKV_STEP_HEADS = 2
ROW_ALIGN = 64
ATTN_ALIGN = 128
FLASH_BLOCKS = 3
PAGES_PER_STEP = 8
VMEM_LIMIT = 56 * 1024 * 1024


def _nt(a, b):
    return lax.dot_general(a, b, (((1,), (1,)), ((), ())), preferred_element_type=F32)


def _tn(a, b):
    return lax.dot_general(a, b, (((0,), (0,)), ((), ())), preferred_element_type=F32)


def _nn(a, b):
    return jnp.dot(a, b, preferred_element_type=F32)


def _pick(n, candidates):
    for c in candidates:
        if n % c == 0:
            return c
    return n


def _mm_kernel(x_ref, w_ref, *rest, act, has_res):
    if has_res:
        r_ref, o_ref, wb_ref = rest
    else:
        o_ref, wb_ref = rest

    @pl.when(pl.program_id(1) == 0)
    def _():
        wb_ref[...] = w_ref[...].astype(BF16)

    acc = _nn(x_ref[...].astype(BF16), wb_ref[...])
    if act is not None:
        acc = act(acc)
    if has_res:
        acc = acc + r_ref[...]
    o_ref[...] = acc.astype(o_ref.dtype)


def _mm(x, w, *, layer=None, act=None, residual=None, out_dtype=F32):
    M, K = x.shape
    N = w.shape[-1]
    tm = _pick(M, (768, 512, 384, 256, 128))
    if K > 4096:
        tm = _pick(M, (384, 256, 128))
    tn = _pick(N, (1024, 512, 256, 128)) if K <= 2048 else _pick(N, (512, 256, 128))
    grid = (N // tn, M // tm)
    if layer is None:
        w_spec = pl.BlockSpec((K, tn), lambda j, i: (0, j))
    else:
        w_spec = pl.BlockSpec((None, K, tn), lambda j, i: (layer, 0, j))
    in_specs = [pl.BlockSpec((tm, K), lambda j, i: (i, 0)), w_spec]
    args = [x, w]
    if residual is not None:
        in_specs.append(pl.BlockSpec((tm, tn), lambda j, i: (i, j)))
        args.append(residual)
    return pl.pallas_call(
        functools.partial(_mm_kernel, act=act, has_res=residual is not None),
        grid=grid,
        in_specs=in_specs,
        out_specs=pl.BlockSpec((tm, tn), lambda j, i: (i, j)),
        out_shape=jax.ShapeDtypeStruct((M, N), out_dtype),
        scratch_shapes=[pltpu.VMEM((K, tn), BF16)],
        compiler_params=pltpu.CompilerParams(
            dimension_semantics=("arbitrary", "arbitrary"), vmem_limit_bytes=VMEM_LIMIT),
    )(*args)


def _bmm_kernel(x_ref, w_ref, o_ref):
    o_ref[0] = _nn(x_ref[0].astype(BF16), w_ref[0].astype(BF16))


def _bmm(x, w):
    G, M, K = x.shape
    N = w.shape[2]
    return pl.pallas_call(
        _bmm_kernel,
        grid=(G,),
        in_specs=[pl.BlockSpec((1, M, K), lambda g: (g, 0, 0)),
                  pl.BlockSpec((1, K, N), lambda g: (g, 0, 0))],
        out_specs=pl.BlockSpec((1, M, N), lambda g: (g, 0, 0)),
        out_shape=jax.ShapeDtypeStruct((G, M, N), F32),
    )(x, w)


def _rms_kernel(x_ref, g_ref, o_ref):
    x = x_ref[...].astype(F32)
    y = x * lax.rsqrt(jnp.mean(x * x, axis=-1, keepdims=True) + RMS_EPS)
    o_ref[...] = (y * g_ref[...]).astype(o_ref.dtype)


def _rmsnorm(x, g, out_dtype):
    M, D = x.shape
    tm = _pick(M, (768, 512, 384, 256, 128))
    return pl.pallas_call(
        _rms_kernel,
        grid=(M // tm,),
        in_specs=[pl.BlockSpec((tm, D), lambda i: (i, 0)),
                  pl.BlockSpec((1, D), lambda i: (0, 0))],
        out_specs=pl.BlockSpec((tm, D), lambda i: (i, 0)),
        out_shape=jax.ShapeDtypeStruct((M, D), out_dtype),
    )(x, g.reshape(1, D).astype(F32))


def _gate(c, p1, p2, z, cw_ref, cb_ref):
    conv = cb_ref[...] + p2 * cw_ref[0:1, :] + p1 * cw_ref[1:2, :] + c * cw_ref[2:3, :]
    return (jax.nn.silu(conv) * z).astype(BF16)


def _ffn_in_prompt_kernel(x_ref, wc_ref, wz_ref, cw_ref, cb_ref, g_ref, tail_ref, wcb_ref, wzb_ref, cs_ref,
                          *, tm, seq_rows, n_seq, tails):
    i = pl.program_id(1)
    S = SUBLANES

    @pl.when(i == 0)
    def _():
        wcb_ref[...] = wc_ref[...].astype(BF16)
        wzb_ref[...] = wz_ref[...].astype(BF16)
        cs_ref[0:S, :] = jnp.zeros((S, cs_ref.shape[1]), F32)

    x = x_ref[...]
    c = _nn(x, wcb_ref[...])
    z = _nn(x, wzb_ref[...])
    cs_ref[S:S + tm, :] = c
    p1 = cs_ref[S - 1:S - 1 + tm, :]
    p2 = cs_ref[S - 2:S - 2 + tm, :]
    row = i * tm + lax.broadcasted_iota(jnp.int32, (tm, 1), 0)
    t = row
    for b in range(1, n_seq):
        t = jnp.where(row >= b * seq_rows, row - b * seq_rows, t)
    p1 = jnp.where(t >= 1, p1, 0.0)
    p2 = jnp.where(t >= 2, p2, 0.0)
    g_ref[...] = _gate(c, p1, p2, z, cw_ref, cb_ref)
    cs_ref[0:S, :] = cs_ref[tm:tm + S, :]
    for b, (tile, off) in enumerate(tails):
        @pl.when(i == tile)
        def _(b=b, off=off):
            tail_ref[b] = c[off:off + S]


def _ffn_in_sample_kernel(x_ref, wc_ref, wz_ref, cw_ref, cb_ref, b0_ref, b1_ref, alias_ref, g_ref, c_ref):
    del alias_ref
    x = x_ref[...]
    c = _nn(x, wc_ref[...].astype(BF16))
    z = _nn(x, wz_ref[...].astype(BF16))
    c_ref[...] = c
    g_ref[...] = _gate(c, b1_ref[...], b0_ref[...], z, cw_ref, cb_ref)


def _ffn_in(xn, w_in, layer, conv_w, conv_b, buf, n_seq, seq_rows, t_valid):
    M, K = xn.shape
    F = conv_b.shape[0]
    MP = n_seq * seq_rows
    DB = M - MP
    S = SUBLANES
    tn = _pick(F, (512, 256, 128))
    nj = F // tn
    tm = _pick(MP, (640, 512, 384, 256, 128))
    cw = conv_w.astype(F32)
    cb = conv_b.reshape(1, F).astype(F32)
    t0 = t_valid - 2
    assert t0 % S <= S - 2 and MP % DB == 0
    tails = []
    for b in range(n_seq):
        r0 = b * seq_rows + (t0 // S) * S
        assert r0 // tm == (r0 + S - 1) // tm
        tails.append((r0 // tm, r0 % tm))
    wspec = lambda off: pl.BlockSpec((None, K, tn), lambda j, i: (layer, 0, j + off))
    cspec = lambda rows: pl.BlockSpec((rows, tn), lambda j, i: (0, j))
    gated, tail = pl.pallas_call(
        functools.partial(_ffn_in_prompt_kernel, tm=tm, seq_rows=seq_rows, n_seq=n_seq, tails=tuple(tails)),
        grid=(nj, MP // tm),
        in_specs=[pl.BlockSpec((tm, K), lambda j, i: (i, 0)), wspec(0), wspec(nj), cspec(3), cspec(1)],
        out_specs=[pl.BlockSpec((tm, tn), lambda j, i: (i, j)),
                   pl.BlockSpec((n_seq, S, tn), lambda j, i: (0, 0, j))],
        out_shape=[jax.ShapeDtypeStruct((M, F), BF16), jax.ShapeDtypeStruct((n_seq, S, F), F32)],
        scratch_shapes=[pltpu.VMEM((K, tn), BF16), pltpu.VMEM((K, tn), BF16), pltpu.VMEM((tm + S, tn), F32)],
        compiler_params=pltpu.CompilerParams(
            dimension_semantics=("arbitrary", "arbitrary"), vmem_limit_bytes=VMEM_LIMIT),
    )(xn, w_in, w_in, cw, cb)
    wspec1 = lambda off: pl.BlockSpec((None, K, tn), lambda j: (layer, 0, j + off))
    cspec1 = lambda rows: pl.BlockSpec((rows, tn), lambda j: (0, j))
    gated, c_s = pl.pallas_call(
        _ffn_in_sample_kernel,
        grid=(nj,),
        in_specs=[pl.BlockSpec((DB, K), lambda j: (MP // DB, 0)), wspec1(0), wspec1(nj), cspec1(3), cspec1(1),
                  cspec1(DB), cspec1(DB), pl.BlockSpec(memory_space=pl.ANY)],
        out_specs=[pl.BlockSpec((DB, tn), lambda j: (MP // DB, j)), cspec1(DB)],
        out_shape=[jax.ShapeDtypeStruct((M, F), BF16), jax.ShapeDtypeStruct((DB, F), F32)],
        input_output_aliases={7: 0},
        compiler_params=pltpu.CompilerParams(dimension_semantics=("arbitrary",), vmem_limit_bytes=VMEM_LIMIT),
    )(xn, w_in, w_in, cw, cb, buf[:, 0], buf[:, 1], gated)
    off = t0 % S
    return gated, tail[:, off:off + 2], c_s


def _wkv_units(r, lw, k, v, a, b, sts, consts, *, chunk, groups, levels):
    bd, code, tri, eye = consts
    W = MXU_DIM
    units = len(sts)
    hi = lw.astype(BF16)
    rem = lw - hi.astype(F32)
    mid = rem.astype(BF16)
    lo = (rem - mid.astype(F32)).astype(BF16)
    lc = _nn(tri, hi) + _nn(tri, mid) + _nn(tri, lo)
    lc_end = lc[chunk - 1:chunk, :]
    e_neg = jnp.exp(-lc)
    e_end = jnp.exp(lc_end - lc)
    d_end = jnp.exp(lc_end)

    def cut(x):
        return [x[:, u * W:(u + 1) * W] for u in range(units)]

    def each(f, *lists):
        return [f(*xs) for xs in zip(*lists)]

    def stack(x):
        return jnp.concatenate([x] * groups, axis=0) * bd

    def stack_b(x):
        return stack(x).astype(BF16)

    bf = lambda x: x.astype(BF16)
    rt_f = each(stack, cut(r * jnp.exp(lc)))
    rt_s = each(bf, rt_f)
    at_s = each(stack_b, cut(a * jnp.exp(lc - lw)))
    kt_s = each(stack_b, cut(k * e_neg))
    bt_s = each(stack_b, cut(b * e_neg))
    kh_s = each(stack_b, cut(k * e_end))
    bh_s = each(stack_b, cut(b * e_end))
    v_s = each(stack_b, cut(v))
    d_end = cut(d_end)

    strict = (code >= 0) & (code < levels)
    incl = code >= 0
    m_ab = each(_nt, at_s, bt_s)
    m_ak = each(lambda x, y: bf(jnp.where(strict, _nt(x, y), 0.0)), at_s, kt_s)
    n_rb = each(lambda x, y: bf(jnp.where(incl, _nt(x, y), 0.0)), rt_s, bt_s)
    n_rk = each(lambda x, y: bf(jnp.where(incl, _nt(x, y), 0.0)), rt_s, kt_s)
    mv_b = each(lambda x, y: bf(_nn(x, y)), m_ak, v_s)

    inv = each(lambda m: eye + jnp.where(code == 0, m, 0.0), m_ab)
    for lev in range(1, levels):
        inv_b = each(bf, inv)
        off = each(lambda m: bf(jnp.where(code == lev, m, 0.0)), m_ab)
        mid_b = each(lambda o, t: bf(_nn(o, t)), off, inv_b)
        inv = each(lambda t, tb, x: t + _nn(tb, x), inv, inv_b, mid_b)
    inv_b = each(bf, inv)

    p_b = each(lambda t, x: bf(_nn(t, x)), inv_b, at_s)
    q_b = each(lambda t, x: bf(_nn(t, x)), inv_b, mv_b)

    g_b = each(lambda d, x, y: bf(eye * d + _tn(x, y)), d_end, bh_s, p_b)
    f_mat = each(lambda x, y, z, w: _tn(x, y) + _tn(z, w), bh_s, q_b, kh_s, v_s)
    ry_b = each(lambda x, n, p: bf(x + _nn(n, p)), rt_f, n_rb, p_b)
    y_0 = each(lambda n, q, m, w: _nn(n, q) + _nn(m, w), n_rb, q_b, n_rk, v_s)

    st_b = each(bf, sts)
    y_s = each(lambda x, s, y0: _nn(x, s) + y0, ry_b, st_b, y_0)
    new_sts = each(lambda g, s, f: _nn(g, s) + f, g_b, st_b, f_mat)

    def unstack(x):
        y = x[0:chunk]
        for h in range(1, groups):
            y = y + x[h * chunk:(h + 1) * chunk]
        return y

    return jnp.concatenate(each(unstack, y_s), axis=1), new_sts


def _wkv_chunk_kernel(r_ref, lw_ref, k_ref, v_ref, a_ref, b_ref, bd_ref, code_ref, tri_ref, eye_ref,
                      y_ref, s_ref, st_ref, *, chunk, groups, units, t_valid, n_chunks, levels):
    c = pl.program_id(1)

    @pl.when(c == 0)
    def _():
        st_ref[...] = jnp.zeros_like(st_ref)

    row = c * chunk + lax.broadcasted_iota(jnp.int32, (chunk, 1), 0)
    valid = row < t_valid
    consts = (bd_ref[...], code_ref[...], tri_ref[...], eye_ref[...])
    ins = (jnp.where(valid, ref[...], 0.0) for ref in (r_ref, lw_ref, k_ref, v_ref, a_ref, b_ref))
    y, sts = _wkv_units(*ins, [st_ref[u] for u in range(units)], consts, chunk=chunk, groups=groups, levels=levels)
    y_ref[...] = y
    for u in range(units):
        st_ref[u] = sts[u]

    @pl.when(c == n_chunks - 1)
    def _():
        s_ref[0] = st_ref[...]


def _wkv_chunked(r, lw, k, v, a, b, n_seq, seq_rows, t_valid, head):
    M, D = r.shape
    L = WKV_CHUNK
    W = MXU_DIM
    U = WKV_UNITS
    G = W // head
    nu = D // (W * U)
    nc = seq_rows // L
    assert seq_rows % L == 0 and D % (W * U) == 0 and G * L == W
    levels = int(np.log2(L))
    assert 2 ** levels == L
    idx = np.arange(W)
    same = (idx[:, None] // L) == (idx[None, :] // L)
    bd = same.astype(np.float32)
    diff = idx[:, None] ^ idx[None, :]
    code = np.floor(np.log2(np.maximum(diff, 1))).astype(np.int32)
    code = np.where(same & (idx[None, :] < idx[:, None]), code, -1)
    code = np.where(idx[None, :] == idx[:, None], levels, code).astype(np.int32)
    tri = jnp.asarray(np.tril(np.ones((L, L), np.float32)), BF16)
    eye = np.eye(W, dtype=np.float32)

    seq = pl.BlockSpec((L, W * U), lambda p, c: ((p // nu) * nc + c, p % nu))
    const = lambda shape: pl.BlockSpec(shape, lambda p, c: (0, 0))
    y, st = pl.pallas_call(
        functools.partial(_wkv_chunk_kernel, chunk=L, groups=G, units=U, t_valid=t_valid, n_chunks=nc,
                          levels=levels),
        grid=(n_seq * nu, nc),
        in_specs=[seq] * 6 + [const((W, W))] * 2 + [const((L, L)), const((W, W))],
        out_specs=[seq, pl.BlockSpec((1, U, W, W), lambda p, c: (p, 0, 0, 0))],
        out_shape=[jax.ShapeDtypeStruct((M, D), F32),
                   jax.ShapeDtypeStruct((n_seq * nu, U, W, W), F32)],
        scratch_shapes=[pltpu.VMEM((U, W, W), F32)],
        compiler_params=pltpu.CompilerParams(dimension_semantics=("arbitrary", "arbitrary")),
    )(r, lw, k, v, a, b, jnp.asarray(bd), jnp.asarray(code), tri, jnp.asarray(eye))
    ng = nu * U
    st = st.reshape(n_seq, ng, G, head, G, head)
    st = jnp.stack([st[:, :, h, :, h, :] for h in range(G)], axis=2)
    return y, jnp.swapaxes(st, -1, -2).reshape(n_seq, ng * G, head, head)


def _wkv_step_kernel(r_ref, lw_ref, k_ref, v_ref, a_ref, b_ref, s_ref, *rest, heads, head):
    y_ref, so_ref = rest[-2:]
    for h in range(heads):
        w, a, b, k, r, v = (ref[h] for ref in (lw_ref, a_ref, b_ref, k_ref, r_ref, v_ref))
        w = jnp.exp(w)
        ys = []
        for i in range(head):
            s = s_ref[h, i]
            sa = jnp.sum(s * a, axis=0, keepdims=True)
            s_new = s * w + sa * b + v[i:i + 1, :] * k
            so_ref[h, i] = s_new
            ys.append(jnp.sum(s_new * r, axis=0, keepdims=True))
        y_ref[h] = jnp.concatenate(ys, axis=0)


def _wkv_step(r, lw, k, v, a, b, state, layer, prev_out):
    H, N, DB = r.shape
    hb = WKV_STEP_HEADS
    assert H % hb == 0
    vec = pl.BlockSpec((hb, N, DB), lambda g: (g, 0, 0))
    mat = pl.BlockSpec((None, hb, N, N, DB), lambda g: (layer, g, 0, 0, 0))
    in_specs = [vec] * 6 + [mat]
    args = [r, lw, k, v, a, b, state]
    aliases = {}
    if prev_out is not None:
        in_specs.append(pl.BlockSpec(memory_space=pl.ANY))
        args.append(prev_out)
        aliases = {7: 1}
    return pl.pallas_call(
        functools.partial(_wkv_step_kernel, heads=hb, head=N),
        grid=(H // hb,),
        in_specs=in_specs,
        out_specs=[vec, mat],
        out_shape=[jax.ShapeDtypeStruct((H, N, DB), F32), jax.ShapeDtypeStruct(state.shape, F32)],
        input_output_aliases=aliases,
        compiler_params=pltpu.CompilerParams(dimension_semantics=("arbitrary",), vmem_limit_bytes=VMEM_LIMIT),
    )(*args)


def _flash_kernel(qi_ref, ki_ref, flag_ref, q_ref, k_ref, v_ref, o_ref, m_sc, acc_sc, *, tq, tk, vdim):
    p = pl.program_id(2)
    qi = qi_ref[p]
    ki = ki_ref[p]
    flags = flag_ref[p]

    @pl.when(ki == 0)
    def _():
        m_sc[...] = jnp.full_like(m_sc, -jnp.inf)
        acc_sc[...] = jnp.zeros_like(acc_sc)

    def update(masked):
        s = _nt(q_ref[0, 0], k_ref[0, 0])
        if masked:
            qpos = qi * tq + lax.broadcasted_iota(jnp.int32, (tq, 1), 0)
            kpos = ki * tk + lax.broadcasted_iota(jnp.int32, (1, tk), 1)
            s = jnp.where(kpos <= qpos, s, -jnp.inf)
        m_old = m_sc[...]
        m_new = jnp.maximum(m_old, jnp.max(s, axis=-1, keepdims=True))
        pm = jnp.exp2(s - m_new).astype(BF16)
        acc_sc[...] = jnp.exp2(m_old - m_new) * acc_sc[...] + _nn(pm, v_ref[0, 0])
        m_sc[...] = m_new

    pl.when((flags & 2) != 0)(lambda: update(True))
    pl.when((flags & 2) == 0)(lambda: update(False))

    @pl.when((flags & 1) != 0)
    def _():
        acc = acc_sc[...]
        o_ref[0] = (acc[:, :vdim] / acc[:, vdim:vdim + 1]).astype(o_ref.dtype)


def _flash(q, k, v, vdim, tq, tk):
    B, H, T, E = q.shape
    VA = v.shape[-1]
    pairs = [(qi, ki) for qi in range(T // tq) for ki in range(T // tk) if ki * tk <= qi * tq + tq - 1]
    n = len(pairs)
    qi_tab = np.array([p[0] for p in pairs], np.int32)
    ki_tab = np.array([p[1] for p in pairs], np.int32)
    flags = np.array([(1 if (i + 1 == n or pairs[i + 1][0] != pairs[i][0]) else 0)
                      + (2 if (ki + 1) * tk - 1 > qi * tq else 0)
                      for i, (qi, ki) in enumerate(pairs)], np.int32)
    grid_spec = pltpu.PrefetchScalarGridSpec(
        num_scalar_prefetch=3,
        grid=(B, H, n),
        in_specs=[pl.BlockSpec((1, 1, tq, E), lambda b, h, p, qt, kt, ft: (b, h, qt[p], 0)),
                  pl.BlockSpec((1, 1, tk, E), lambda b, h, p, qt, kt, ft: (b, h, kt[p], 0)),
                  pl.BlockSpec((1, 1, tk, VA), lambda b, h, p, qt, kt, ft: (b, h, kt[p], 0))],
        out_specs=pl.BlockSpec((1, tq, vdim), lambda b, h, p, qt, kt, ft: (b, qt[p], h)),
        scratch_shapes=[pltpu.VMEM((tq, 1), F32), pltpu.VMEM((tq, VA), F32)],
    )
    return pl.pallas_call(
        functools.partial(_flash_kernel, tq=tq, tk=tk, vdim=vdim),
        grid_spec=grid_spec,
        out_shape=jax.ShapeDtypeStruct((B, T, H * vdim), BF16),
        compiler_params=pltpu.CompilerParams(
            dimension_semantics=("arbitrary", "arbitrary", "arbitrary"), vmem_limit_bytes=VMEM_LIMIT),
    )(jnp.asarray(qi_tab), jnp.asarray(ki_tab), jnp.asarray(flags), q, k, v)


def _paged_kernel(pt_ref, *refs, heads, nope, qk_dim, n_steps, pages, have_rinv):
    del pt_ref
    c_refs, p_refs = refs[:pages], refs[pages:2 * pages]
    qa_ref, qp_ref, x_ref = refs[2 * pages:2 * pages + 3]
    rest = refs[2 * pages + 3:]
    if have_rinv:
        acc_ref, m_ref, l_ref, m_sc, l_sc, acc_sc = rest
    else:
        acc_ref, m_ref, l_ref, rinv_ref, m_sc, l_sc, acc_sc = rest
    step = pl.program_id(1)

    @pl.when(step == 0)
    def _():
        m_sc[...] = jnp.full_like(m_sc, -jnp.inf)
        l_sc[...] = jnp.zeros_like(l_sc)
        acc_sc[...] = jnp.zeros_like(acc_sc)

    c = jnp.concatenate([r[0] for r in c_refs], axis=0).astype(BF16)
    kp = jnp.concatenate([r[0] for r in p_refs], axis=1)
    tokens = c.shape[0]
    if have_rinv:
        rinv = x_ref[0, 0]
    else:
        sub = MXU_DIM
        parts = []
        for t in range(0, tokens, sub):
            knt = _nt(x_ref[...], c[t:t + sub])
            parts.append(jnp.sum((knt * knt).reshape(heads, nope, sub), axis=1))
        ssq = jnp.concatenate(parts, axis=1)
        kss = jnp.sum(kp * kp, axis=0, keepdims=True)
        rinv = lax.rsqrt((ssq + kss) * (1.0 / qk_dim) + RMS_EPS)
        rinv_ref[0, 0] = rinv
    s = (_nt(qa_ref[0].astype(BF16), c) + _nn(qp_ref[0].astype(BF16), kp.astype(BF16))) * rinv
    m_old = m_sc[...]
    m_new = jnp.maximum(m_old, jnp.max(s, axis=-1, keepdims=True))
    alpha = jnp.exp(m_old - m_new)
    pm = jnp.exp(s - m_new)
    l_sc[...] = alpha * l_sc[...] + jnp.sum(pm, axis=-1, keepdims=True)
    acc_sc[...] = alpha * acc_sc[...] + _nn(pm.astype(BF16), c)
    m_sc[...] = m_new

    @pl.when(step == n_steps - 1)
    def _():
        acc_ref[0] = acc_sc[...]
        m_ref[0] = jnp.broadcast_to(m_sc[...], m_ref.shape[1:])
        l_ref[0] = jnp.broadcast_to(l_sc[...], l_ref.shape[1:])


def _paged_attn(page_table, cache_ckv, cache_kpe_t, qa, qp, nope, *, wnt=None, rinv=None):
    DB, n_pages = page_table.shape
    _, page, R = cache_ckv.shape
    rope = cache_kpe_t.shape[1]
    H = qa.shape[1]
    P = PAGES_PER_STEP
    assert n_pages % P == 0
    n_steps = n_pages // P
    have_rinv = rinv is not None
    page_spec = lambda shape, i: pl.BlockSpec((1,) + shape, lambda b, s, pt: (pt[b, P * s + i], 0, 0))
    rinv_spec = pl.BlockSpec((1, 1, H, P * page), lambda b, s, pt: (b, s, 0, 0))
    x_spec = rinv_spec if have_rinv else pl.BlockSpec(wnt.shape, lambda b, s, pt: (0, 0))
    stat_spec = pl.BlockSpec((1, H, LANES), lambda b, s, pt: (b, 0, 0))
    out_specs = [pl.BlockSpec((1, H, R), lambda b, s, pt: (b, 0, 0)), stat_spec, stat_spec]
    out_shape = [jax.ShapeDtypeStruct((DB, H, R), F32),
                 jax.ShapeDtypeStruct((DB, H, LANES), F32),
                 jax.ShapeDtypeStruct((DB, H, LANES), F32)]
    if not have_rinv:
        out_specs.append(rinv_spec)
        out_shape.append(jax.ShapeDtypeStruct((DB, n_steps, H, P * page), F32))
    grid_spec = pltpu.PrefetchScalarGridSpec(
        num_scalar_prefetch=1,
        grid=(DB, n_steps),
        in_specs=([page_spec((page, R), i) for i in range(P)] + [page_spec((rope, page), i) for i in range(P)]
                  + [pl.BlockSpec((1, H, R), lambda b, s, pt: (b, 0, 0)),
                     pl.BlockSpec((1, H, rope), lambda b, s, pt: (b, 0, 0)), x_spec]),
        out_specs=out_specs,
        scratch_shapes=[pltpu.VMEM((H, 1), F32), pltpu.VMEM((H, 1), F32), pltpu.VMEM((H, R), F32)],
    )
    outs = pl.pallas_call(
        functools.partial(_paged_kernel, heads=H, nope=nope, qk_dim=nope + rope, n_steps=n_steps, pages=P,
                          have_rinv=have_rinv),
        grid_spec=grid_spec,
        out_shape=out_shape,
        compiler_params=pltpu.CompilerParams(
            dimension_semantics=("arbitrary", "arbitrary"), vmem_limit_bytes=VMEM_LIMIT),
    )(page_table, *([cache_ckv] * P), *([cache_kpe_t] * P), qa, qp, rinv if have_rinv else wnt)
    acc, m, l = outs[:3]
    return acc, m[:, :, 0], l[:, :, 0], (rinv if have_rinv else outs[3])


def _pad_cols(w, n):
    return jnp.pad(w, ((0, 0), (0, n - w.shape[1])))


def _pad_rows(w, n):
    return jnp.pad(w, ((0, n - w.shape[0]), (0, 0)))


def _lora(x, w1, w2, act):
    rank = w1.shape[1]
    rp = -(-rank // LANES) * LANES
    mid = _mm(x, _pad_cols(w1, rp), act=act, out_dtype=BF16)
    return _mm(mid, _pad_rows(w2, rp))


def _rope(x, cos, sin):
    half = x.shape[-1] // 2
    x1, x2 = x[..., :half], x[..., half:]
    return jnp.concatenate([x1 * cos - x2 * sin, x2 * cos + x1 * sin], axis=-1)


def _head_rms(x, g):
    return x * lax.rsqrt(jnp.mean(x * x, axis=-1, keepdims=True) + RMS_EPS) * g


def kernel(x_prompt, x_sample, state_shift, state_wkv, state_conv, cache_ckv, cache_kpe, page_table, meta_tokens, norm_mix, norm_ffn, mu, w_rkv, w_o_a, w0, w1, w2, a0, a1, a2, v0, v1, v2, g1, g2, k_k, k_a, r_k, lnx_w, lnx_b, ffn_w_in, ffn_conv_w, ffn_conv_b, ffn_w_out, norm_kv, w_dkv, g_ckv, w_ukv, g_k, w_dq, g_q, w_uq, g_qn, w_o_b):
    B, seq, D = x_prompt.shape
    DB = x_sample.shape[0]
    assert x_sample.shape[1] == 1
    n_meta = meta_tokens.shape[0]
    depth = norm_mix.shape[0]
    n_a = mu.shape[0]
    HA, NA = r_k.shape[1], r_k.shape[2]
    assert ffn_conv_w.shape[1] == 3
    R = g_ckv.shape[0]
    rope = w_dkv.shape[1] - R
    HB = w_ukv.shape[1]
    nope = g_k.shape[0] - rope // 2
    vdim = w_ukv.shape[2] - nope
    qk = nope + rope
    T = seq + n_meta
    Tp = -(-T // ROW_ALIGN) * ROW_ALIGN
    Tpa = -(-Tp // ATTN_ALIGN) * ATTN_ALIGN
    MP = B * Tp
    M = MP + DB
    past_len = page_table.shape[1] * cache_ckv.shape[1]
    scale = qk ** -0.5

    h0 = jnp.concatenate([jnp.broadcast_to(meta_tokens[None], (B, n_meta, D)), x_prompt], axis=1)
    h0 = jnp.pad(h0, ((0, 0), (0, Tp - T), (0, 0)))
    h = jnp.concatenate([h0.reshape(MP, D), x_sample.reshape(DB, D)], axis=0)

    t_of_row = jnp.concatenate([jnp.tile(jnp.arange(Tp), B), jnp.full((DB,), past_len)])
    not_first = (t_of_row[:MP] > 0).astype(F32)[:, None]
    inv_freq = ROPE_THETA ** (-jnp.arange(0, rope, 2, dtype=F32) / rope)
    ang = t_of_row.astype(F32)[:, None] * inv_freq[None]
    cos, sin = jnp.cos(ang), jnp.sin(ang)
    last_rows = np.array([b * Tp + T - 1 for b in range(B)])

    state_wkv_t = jnp.transpose(state_wkv, (0, 2, 3, 4, 1))
    cache_kpe_t = jnp.swapaxes(cache_kpe, 1, 2)
    w_rkv3 = w_rkv.reshape(n_a * 3, D, D)
    w_uq3 = w_uq.reshape(w_uq.shape[0], w_uq.shape[1], HB * qk)
    w_ob3 = w_o_b.reshape(w_o_b.shape[0], HB * vdim, D)

    shift_p, shift_s, wkv_p, conv_p, conv_s = [], [], [], [], []
    wkv_s_t = None
    v_first = None
    kv_p = kv_s = None
    ckv = kpe = None

    for i in range(depth):
        if i < n_a:
            xn = _rmsnorm(h, norm_mix[i], F32)
            shift_p.append(xn[last_rows])
            shift_s.append(xn[MP:])
            prev = jnp.concatenate([jnp.roll(xn[:MP], 1, axis=0) * not_first, state_shift[i]], axis=0)
            dx = prev - xn
            xs = [(xn + dx * mu[i, s]).astype(BF16) for s in range(6)]
            r = _mm(xs[0], w_rkv3, layer=3 * i)
            k = _mm(xs[1], w_rkv3, layer=3 * i + 1)
            v = _mm(xs[2], w_rkv3, layer=3 * i + 2)
            w_log = -jax.nn.softplus(-(w0[i] + _lora(xs[3], w1[i], w2[i], jnp.tanh))) - 0.5
            log_decay = -jnp.exp(w_log)
            if i > 0:
                v = v + (v_first - v) * jax.nn.sigmoid(v0[i - 1] + _lora(xs[2], v1[i - 1], v2[i - 1], None))
            else:
                v_first = v
            a = jax.nn.sigmoid(a0[i] + _lora(xs[4], a1[i], a2[i], None))
            g = _lora(xs[5], g1[i], g2[i], jax.nn.sigmoid)
            kk = (k * k_k[i]).reshape(M, HA, NA)
            kk = (kk / jnp.maximum(jnp.sqrt(jnp.sum(kk * kk, axis=-1, keepdims=True)), 1e-12)).reshape(M, D)
            k = k * (1.0 + (a - 1.0) * k_a[i])
            sa, sb = -kk, kk * a
            y, st_p = _wkv_chunked(r, log_decay, k, v, sa, sb, B, Tp, T, NA)
            lanes_t = lambda t: t[MP:].reshape(DB, HA, NA).transpose(1, 2, 0)
            y_s, wkv_s_t = _wkv_step(*(lanes_t(t) for t in (r, log_decay, k, v, sa, sb)), state_wkv_t, i, wkv_s_t)
            y = lax.dynamic_update_slice(y, y_s.transpose(2, 0, 1).reshape(DB, D), (MP, 0))
            wkv_p.append(st_p)
            y = y.reshape(M, HA, NA)
            mean = jnp.mean(y, axis=-1, keepdims=True)
            var = jnp.mean(jnp.square(y - mean), axis=-1, keepdims=True)
            yn = ((y - mean) * lax.rsqrt(var + NA * GN_EPS_PER_CHANNEL)).reshape(M, D) * lnx_w[i] + lnx_b[i]
            bonus = jnp.sum((r * k).reshape(M, HA, NA) * r_k[i], axis=-1, keepdims=True) * v.reshape(M, HA, NA)
            h = _mm(((yn + bonus.reshape(M, D)) * g).astype(BF16), w_o_a, layer=i, residual=h)
        else:
            j = i - n_a
            xn = _rmsnorm(h, norm_mix[i], BF16)
            cq = _rmsnorm(_mm(xn, w_dq, layer=j), g_q[j], BF16)
            q = _mm(cq, w_uq3, layer=j).reshape(M, HB, qk)
            q = jnp.concatenate([q[..., :nope], _rope(q[..., nope:], cos[:, None], sin[:, None])], axis=-1)
            gq = g_qn[j]
            q = _head_rms(q, jnp.concatenate([gq[:nope], gq[nope:], gq[nope:]])) * scale
            q_p, q_s = (q[:MP] * LOG2E).reshape(B, Tp, HB, qk), q[MP:]
            q_p = jnp.pad(q_p.transpose(0, 2, 1, 3), ((0, 0), (0, 0), (0, Tpa - Tp), (0, 0))).astype(BF16)
            o_p = _flash(q_p, kv_p[0], kv_p[1], vdim, tq=Tpa // FLASH_BLOCKS, tk=Tpa // FLASH_BLOCKS)[:, :Tp]
            qn = q_s[..., :nope] * g_k[:nope]
            qp = q_s[..., nope:] * jnp.concatenate([g_k[nope:], g_k[nope:]])
            qa = _bmm(qn.transpose(1, 0, 2), kv_s['wnt3']).transpose(1, 0, 2)
            acc, m, l, kv_s['rinv_cache'] = _paged_attn(
                page_table, cache_ckv, cache_kpe_t, qa, qp, nope, wnt=kv_s['wnt'], rinv=kv_s['rinv_cache'])
            c_new, kp_new = kv_s['ckv'], kv_s['kpe']
            s_new = kv_s['rinv'] * (jnp.sum(qa * c_new[:, None, :], axis=-1) + jnp.sum(qp * kp_new[:, None, :], axis=-1))
            m_f = jnp.maximum(m, s_new)
            alpha = jnp.exp(m - m_f)
            pn = jnp.exp(s_new - m_f)
            l_f = l * alpha + pn
            ctx = (acc * alpha[..., None] + pn[..., None] * c_new[:, None, :]) / l_f[..., None]
            o_s = _bmm(ctx.transpose(1, 0, 2), kv_s['wv']).transpose(1, 0, 2).reshape(DB, HB * vdim)
            attn = jnp.concatenate([o_p.reshape(MP, HB * vdim), o_s.astype(BF16)], axis=0)
            h = _mm(attn, w_ob3, layer=j, residual=h)

        xn = _rmsnorm(h, norm_ffn[i], BF16)
        gated, c_tail, c_s = _ffn_in(xn, ffn_w_in, i, ffn_conv_w[i], ffn_conv_b[i], state_conv[i], B, Tp, T)
        conv_p.append(c_tail)
        conv_s.append(jnp.stack([state_conv[i][:, 1], c_s], axis=1))
        h = _mm(gated, ffn_w_out, layer=i, residual=h)

        if i == n_a - 1:
            xk = _rmsnorm(h, norm_kv, BF16)
            ckv = _rmsnorm(_mm(xk, w_dkv[:, :R]), g_ckv, F32)
            kpe = _rope(_mm(xk, w_dkv[:, R:]), cos, sin)
            kv = _mm(ckv.astype(BF16), w_ukv.reshape(R, HB * (nope + vdim))).reshape(M, HB, nope + vdim)
            k_nope = kv[..., :nope]
            ssq = jnp.sum(k_nope * k_nope, axis=-1) + jnp.sum(kpe * kpe, axis=-1, keepdims=True)
            rinv = lax.rsqrt(ssq / qk + RMS_EPS)
            gk_full = jnp.concatenate([g_k[:nope], g_k[nope:], g_k[nope:]])
            k_full = jnp.concatenate([k_nope, jnp.broadcast_to(kpe[:, None, :], (M, HB, rope))], axis=-1)
            k_full = k_full * rinv[..., None] * gk_full
            v_aug = jnp.concatenate([kv[..., nope:], jnp.ones((M, HB, 1), F32), jnp.zeros((M, HB, vdim - 1), F32)],
                                    axis=-1)
            to_heads = lambda t: jnp.pad(t[:MP].reshape(B, Tp, HB, -1).transpose(0, 2, 1, 3),
                                         ((0, 0), (0, 0), (0, Tpa - Tp), (0, 0))).astype(BF16)
            kv_p = (to_heads(k_full), to_heads(v_aug))
            wnt3 = w_ukv[:, :, :nope].transpose(1, 2, 0)
            kv_s = dict(ckv=ckv[MP:], kpe=kpe[MP:], rinv=rinv[MP:], wnt3=wnt3, rinv_cache=None,
                        wnt=wnt3.reshape(HB * nope, R).astype(BF16),
                        wv=w_ukv[:, :, nope:].transpose(1, 0, 2))

    wkv_s = jnp.transpose(wkv_s_t, (0, 4, 1, 2, 3))
    return (h[:MP].reshape(B, Tp, D)[:, n_meta:T], h[MP:].reshape(DB, 1, D),
            jnp.stack(shift_p), jnp.stack(wkv_p), jnp.stack(conv_p),
            ckv[:MP].reshape(B, Tp, R)[:, :T], kpe[:MP].reshape(B, Tp, rope)[:, :T],
            jnp.stack(shift_s), wkv_s, jnp.stack(conv_s),
            ckv[MP:].reshape(DB, 1, R), kpe[MP:].reshape(DB, 1, rope))
```

```python
import functools

import numpy as np
import jax
import jax.numpy as jnp
from jax import lax
from jax.experimental import pallas as pl
from jax.experimental.pallas import tpu as pltpu

F32 = jnp.float32
BF16 = jnp.bfloat16

RMS_EPS = 1e-6
ROPE_THETA = 10000.0
GN_EPS_PER_CHANNEL = 1e-5
LOG2E = 1.4426950408889634
LANES = 128
SUBLANES = 8
MXU_DIM = 256
WKV_CHUNK = 64
WKV_UNITS = 4
WKV_STEP_HEADS = 2
ROW_ALIGN = 64
ATTN_ALIGN = 128
FLASH_BLOCKS = 3
FLASH_HEADS = 2
PAGES_PER_STEP = 8
VMEM_LIMIT = 56 * 1024 * 1024


def _nt(a, b):
    return lax.dot_general(a, b, (((1,), (1,)), ((), ())), preferred_element_type=F32)


def _tn(a, b):
    return lax.dot_general(a, b, (((0,), (0,)), ((), ())), preferred_element_type=F32)


def _nn(a, b):
    return jnp.dot(a, b, preferred_element_type=F32)


def _pick(n, candidates):
    for c in candidates:
        if n % c == 0:
            return c
    return n


def _mm_kernel(x_ref, w_ref, *rest, act, has_res):
    if has_res:
        r_ref, o_ref, wb_ref = rest
    else:
        o_ref, wb_ref = rest

    @pl.when(pl.program_id(1) == 0)
    def _():
        wb_ref[...] = w_ref[...].astype(BF16)

    acc = _nn(x_ref[...].astype(BF16), wb_ref[...])
    if act is not None:
        acc = act(acc)
    if has_res:
        acc = acc + r_ref[...]
    o_ref[...] = acc.astype(o_ref.dtype)


def _mm(x, w, *, layer=None, act=None, residual=None, out_dtype=F32):
    M, K = x.shape
    N = w.shape[-1]
    tm = _pick(M, (768, 512, 384, 256, 128))
    if K > 4096:
        tm = _pick(M, (384, 256, 128))
    tn = _pick(N, (1024, 512, 256, 128)) if K <= 2048 else _pick(N, (512, 256, 128))
    grid = (N // tn, M // tm)
    if layer is None:
        w_spec = pl.BlockSpec((K, tn), lambda j, i: (0, j))
    else:
        w_spec = pl.BlockSpec((None, K, tn), lambda j, i: (layer, 0, j))
    in_specs = [pl.BlockSpec((tm, K), lambda j, i: (i, 0)), w_spec]
    args = [x, w]
    if residual is not None:
        in_specs.append(pl.BlockSpec((tm, tn), lambda j, i: (i, j)))
        args.append(residual)
    return pl.pallas_call(
        functools.partial(_mm_kernel, act=act, has_res=residual is not None),
        grid=grid,
        in_specs=in_specs,
        out_specs=pl.BlockSpec((tm, tn), lambda j, i: (i, j)),
        out_shape=jax.ShapeDtypeStruct((M, N), out_dtype),
        scratch_shapes=[pltpu.VMEM((K, tn), BF16)],
        compiler_params=pltpu.CompilerParams(
            dimension_semantics=("arbitrary", "arbitrary"), vmem_limit_bytes=VMEM_LIMIT),
    )(*args)


def _bmm_kernel(x_ref, w_ref, o_ref):
    o_ref[0] = _nn(x_ref[0].astype(BF16), w_ref[0].astype(BF16))


def _bmm(x, w):
    G, M, K = x.shape
    N = w.shape[2]
    return pl.pallas_call(
        _bmm_kernel,
        grid=(G,),
        in_specs=[pl.BlockSpec((1, M, K), lambda g: (g, 0, 0)),
                  pl.BlockSpec((1, K, N), lambda g: (g, 0, 0))],
        out_specs=pl.BlockSpec((1, M, N), lambda g: (g, 0, 0)),
        out_shape=jax.ShapeDtypeStruct((G, M, N), F32),
    )(x, w)


def _rms_kernel(x_ref, g_ref, o_ref):
    x = x_ref[...].astype(F32)
    y = x * lax.rsqrt(jnp.mean(x * x, axis=-1, keepdims=True) + RMS_EPS)
    o_ref[...] = (y * g_ref[...]).astype(o_ref.dtype)


def _rmsnorm(x, g, out_dtype):
    M, D = x.shape
    tm = _pick(M, (768, 512, 384, 256, 128))
    return pl.pallas_call(
        _rms_kernel,
        grid=(M // tm,),
        in_specs=[pl.BlockSpec((tm, D), lambda i: (i, 0)),
                  pl.BlockSpec((1, D), lambda i: (0, 0))],
        out_specs=pl.BlockSpec((tm, D), lambda i: (i, 0)),
        out_shape=jax.ShapeDtypeStruct((M, D), out_dtype),
    )(x, g.reshape(1, D).astype(F32))


def _gate(c, p1, p2, z, cw_ref, cb_ref):
    conv = cb_ref[...] + p2 * cw_ref[0:1, :] + p1 * cw_ref[1:2, :] + c * cw_ref[2:3, :]
    return (jax.nn.silu(conv) * z).astype(BF16)


def _ffn_in_prompt_kernel(x_ref, wc_ref, wz_ref, cw_ref, cb_ref, g_ref, tail_ref, wcb_ref, wzb_ref, cs_ref,
                          *, tm, seq_rows, n_seq, tails):
    i = pl.program_id(1)
    S = SUBLANES

    @pl.when(i == 0)
    def _():
        wcb_ref[...] = wc_ref[...].astype(BF16)
        wzb_ref[...] = wz_ref[...].astype(BF16)
        cs_ref[0:S, :] = jnp.zeros((S, cs_ref.shape[1]), F32)

    x = x_ref[...]
    c = _nn(x, wcb_ref[...])
    z = _nn(x, wzb_ref[...])
    cs_ref[S:S + tm, :] = c
    p1 = cs_ref[S - 1:S - 1 + tm, :]
    p2 = cs_ref[S - 2:S - 2 + tm, :]
    row = i * tm + lax.broadcasted_iota(jnp.int32, (tm, 1), 0)
    t = row
    for b in range(1, n_seq):
        t = jnp.where(row >= b * seq_rows, row - b * seq_rows, t)
    p1 = jnp.where(t >= 1, p1, 0.0)
    p2 = jnp.where(t >= 2, p2, 0.0)
    g_ref[...] = _gate(c, p1, p2, z, cw_ref, cb_ref)
    cs_ref[0:S, :] = cs_ref[tm:tm + S, :]
    for b, (tile, off) in enumerate(tails):
        @pl.when(i == tile)
        def _(b=b, off=off):
            tail_ref[b] = c[off:off + S]


def _ffn_in_sample_kernel(x_ref, wc_ref, wz_ref, cw_ref, cb_ref, b0_ref, b1_ref, alias_ref, g_ref, c_ref):
    del alias_ref
    x = x_ref[...]
    c = _nn(x, wc_ref[...].astype(BF16))
    z = _nn(x, wz_ref[...].astype(BF16))
    c_ref[...] = c
    g_ref[...] = _gate(c, b1_ref[...], b0_ref[...], z, cw_ref, cb_ref)


def _ffn_in(xn, w_in, layer, conv_w, conv_b, buf, n_seq, seq_rows, t_valid):
    M, K = xn.shape
    F = conv_b.shape[0]
    MP = n_seq * seq_rows
    DB = M - MP
    S = SUBLANES
    tn = _pick(F, (512, 256, 128))
    nj = F // tn
    tm = _pick(MP, (640, 512, 384, 256, 128))
    cw = conv_w.astype(F32)
    cb = conv_b.reshape(1, F).astype(F32)
    t0 = t_valid - 2
    assert t0 % S <= S - 2 and MP % DB == 0
    tails = []
    for b in range(n_seq):
        r0 = b * seq_rows + (t0 // S) * S
        assert r0 // tm == (r0 + S - 1) // tm
        tails.append((r0 // tm, r0 % tm))
    wspec = lambda off: pl.BlockSpec((None, K, tn), lambda j, i: (layer, 0, j + off))
    cspec = lambda rows: pl.BlockSpec((rows, tn), lambda j, i: (0, j))
    gated, tail = pl.pallas_call(
        functools.partial(_ffn_in_prompt_kernel, tm=tm, seq_rows=seq_rows, n_seq=n_seq, tails=tuple(tails)),
        grid=(nj, MP // tm),
        in_specs=[pl.BlockSpec((tm, K), lambda j, i: (i, 0)), wspec(0), wspec(nj), cspec(3), cspec(1)],
        out_specs=[pl.BlockSpec((tm, tn), lambda j, i: (i, j)),
                   pl.BlockSpec((n_seq, S, tn), lambda j, i: (0, 0, j))],
        out_shape=[jax.ShapeDtypeStruct((M, F), BF16), jax.ShapeDtypeStruct((n_seq, S, F), F32)],
        scratch_shapes=[pltpu.VMEM((K, tn), BF16), pltpu.VMEM((K, tn), BF16), pltpu.VMEM((tm + S, tn), F32)],
        compiler_params=pltpu.CompilerParams(
            dimension_semantics=("arbitrary", "arbitrary"), vmem_limit_bytes=VMEM_LIMIT),
    )(xn, w_in, w_in, cw, cb)
    wspec1 = lambda off: pl.BlockSpec((None, K, tn), lambda j: (layer, 0, j + off))
    cspec1 = lambda rows: pl.BlockSpec((rows, tn), lambda j: (0, j))
    gated, c_s = pl.pallas_call(
        _ffn_in_sample_kernel,
        grid=(nj,),
        in_specs=[pl.BlockSpec((DB, K), lambda j: (MP // DB, 0)), wspec1(0), wspec1(nj), cspec1(3), cspec1(1),
                  cspec1(DB), cspec1(DB), pl.BlockSpec(memory_space=pl.ANY)],
        out_specs=[pl.BlockSpec((DB, tn), lambda j: (MP // DB, j)), cspec1(DB)],
        out_shape=[jax.ShapeDtypeStruct((M, F), BF16), jax.ShapeDtypeStruct((DB, F), F32)],
        input_output_aliases={7: 0},
        compiler_params=pltpu.CompilerParams(dimension_semantics=("arbitrary",), vmem_limit_bytes=VMEM_LIMIT),
    )(xn, w_in, w_in, cw, cb, buf[:, 0], buf[:, 1], gated)
    off = t0 % S
    return gated, tail[:, off:off + 2], c_s


def _wkv_units(r, lw, k, v, a, b, sts, consts, *, chunk, groups, levels):
    bd, code, tri, eye = consts
    W = MXU_DIM
    units = len(sts)
    hi = lw.astype(BF16)
    rem = lw - hi.astype(F32)
    mid = rem.astype(BF16)
    lo = (rem - mid.astype(F32)).astype(BF16)
    lc = _nn(tri, hi) + _nn(tri, mid) + _nn(tri, lo)
    lc_end = lc[chunk - 1:chunk, :]
    e_neg = jnp.exp(-lc)
    e_end = jnp.exp(lc_end - lc)
    d_end = jnp.exp(lc_end)

    def cut(x):
        return [x[:, u * W:(u + 1) * W] for u in range(units)]

    def each(f, *lists):
        return [f(*xs) for xs in zip(*lists)]

    def stack(x):
        return jnp.concatenate([x] * groups, axis=0) * bd

    def stack_b(x):
        return stack(x).astype(BF16)

    bf = lambda x: x.astype(BF16)
    rt_f = each(stack, cut(r * jnp.exp(lc)))
    rt_s = each(bf, rt_f)
    at_s = each(stack_b, cut(a * jnp.exp(lc - lw)))
    kt_s = each(stack_b, cut(k * e_neg))
    bt_s = each(stack_b, cut(b * e_neg))
    kh_s = each(stack_b, cut(k * e_end))
    bh_s = each(stack_b, cut(b * e_end))
    v_s = each(stack_b, cut(v))
    d_end = cut(d_end)

    strict = (code >= 0) & (code < levels)
    incl = code >= 0
    m_ab = each(_nt, at_s, bt_s)
    m_ak = each(lambda x, y: bf(jnp.where(strict, _nt(x, y), 0.0)), at_s, kt_s)
    n_rb = each(lambda x, y: bf(jnp.where(incl, _nt(x, y), 0.0)), rt_s, bt_s)
    n_rk = each(lambda x, y: bf(jnp.where(incl, _nt(x, y), 0.0)), rt_s, kt_s)
    mv_b = each(lambda x, y: bf(_nn(x, y)), m_ak, v_s)

    inv = each(lambda m: eye + jnp.where(code == 0, m, 0.0), m_ab)
    for lev in range(1, levels):
        inv_b = each(bf, inv)
        off = each(lambda m: bf(jnp.where(code == lev, m, 0.0)), m_ab)
        mid_b = each(lambda o, t: bf(_nn(o, t)), off, inv_b)
        inv = each(lambda t, tb, x: t + _nn(tb, x), inv, inv_b, mid_b)
    inv_b = each(bf, inv)

    p_b = each(lambda t, x: bf(_nn(t, x)), inv_b, at_s)
    q_b = each(lambda t, x: bf(_nn(t, x)), inv_b, mv_b)

    g_b = each(lambda d, x, y: bf(eye * d + _tn(x, y)), d_end, bh_s, p_b)
    f_mat = each(lambda x, y, z, w: _tn(x, y) + _tn(z, w), bh_s, q_b, kh_s, v_s)
    ry_b = each(lambda x, n, p: bf(x + _nn(n, p)), rt_f, n_rb, p_b)
    y_0 = each(lambda n, q, m, w: _nn(n, q) + _nn(m, w), n_rb, q_b, n_rk, v_s)

    st_b = each(bf, sts)
    y_s = each(lambda x, s, y0: _nn(x, s) + y0, ry_b, st_b, y_0)
    new_sts = each(lambda g, s, f: _nn(g, s) + f, g_b, st_b, f_mat)

    def unstack(x):
        y = x[0:chunk]
        for h in range(1, groups):
            y = y + x[h * chunk:(h + 1) * chunk]
        return y

    return jnp.concatenate(each(unstack, y_s), axis=1), new_sts


def _wkv_chunk_kernel(r_ref, lw_ref, k_ref, v_ref, a_ref, b_ref, bd_ref, code_ref, tri_ref, eye_ref,
                      y_ref, s_ref, st_ref, *, chunk, groups, units, t_valid, n_chunks, levels):
    c = pl.program_id(1)

    @pl.when(c == 0)
    def _():
        st_ref[...] = jnp.zeros_like(st_ref)

    row = c * chunk + lax.broadcasted_iota(jnp.int32, (chunk, 1), 0)
    valid = row < t_valid
    consts = (bd_ref[...], code_ref[...], tri_ref[...], eye_ref[...])
    ins = (jnp.where(valid, ref[...], 0.0) for ref in (r_ref, lw_ref, k_ref, v_ref, a_ref, b_ref))
    y, sts = _wkv_units(*ins, [st_ref[u] for u in range(units)], consts, chunk=chunk, groups=groups, levels=levels)
    y_ref[...] = y
    for u in range(units):
        st_ref[u] = sts[u]

    @pl.when(c == n_chunks - 1)
    def _():
        s_ref[0] = st_ref[...]


def _wkv_chunked(r, lw, k, v, a, b, n_seq, seq_rows, t_valid, head):
    M, D = r.shape
    L = WKV_CHUNK
    W = MXU_DIM
    U = WKV_UNITS
    G = W // head
    nu = D // (W * U)
    nc = seq_rows // L
    assert seq_rows % L == 0 and D % (W * U) == 0 and G * L == W
    levels = int(np.log2(L))
    assert 2 ** levels == L
    idx = np.arange(W)
    same = (idx[:, None] // L) == (idx[None, :] // L)
    bd = same.astype(np.float32)
    diff = idx[:, None] ^ idx[None, :]
    code = np.floor(np.log2(np.maximum(diff, 1))).astype(np.int32)
    code = np.where(same & (idx[None, :] < idx[:, None]), code, -1)
    code = np.where(idx[None, :] == idx[:, None], levels, code).astype(np.int32)
    tri = jnp.asarray(np.tril(np.ones((L, L), np.float32)), BF16)
    eye = np.eye(W, dtype=np.float32)

    seq = pl.BlockSpec((L, W * U), lambda p, c: ((p // nu) * nc + c, p % nu))
    const = lambda shape: pl.BlockSpec(shape, lambda p, c: (0, 0))
    y, st = pl.pallas_call(
        functools.partial(_wkv_chunk_kernel, chunk=L, groups=G, units=U, t_valid=t_valid, n_chunks=nc,
                          levels=levels),
        grid=(n_seq * nu, nc),
        in_specs=[seq] * 6 + [const((W, W))] * 2 + [const((L, L)), const((W, W))],
        out_specs=[seq, pl.BlockSpec((1, U, W, W), lambda p, c: (p, 0, 0, 0))],
        out_shape=[jax.ShapeDtypeStruct((M, D), F32),
                   jax.ShapeDtypeStruct((n_seq * nu, U, W, W), F32)],
        scratch_shapes=[pltpu.VMEM((U, W, W), F32)],
        compiler_params=pltpu.CompilerParams(dimension_semantics=("arbitrary", "arbitrary")),
    )(r, lw, k, v, a, b, jnp.asarray(bd), jnp.asarray(code), tri, jnp.asarray(eye))
    ng = nu * U
    st = st.reshape(n_seq, ng, G, head, G, head)
    st = jnp.stack([st[:, :, h, :, h, :] for h in range(G)], axis=2)
    return y, jnp.swapaxes(st, -1, -2).reshape(n_seq, ng * G, head, head)


def _wkv_step_kernel(r_ref, lw_ref, k_ref, v_ref, a_ref, b_ref, s_ref, *rest, heads, head):
    y_ref, so_ref = rest[-2:]
    for h in range(heads):
        w, a, b, k, r, v = (ref[h] for ref in (lw_ref, a_ref, b_ref, k_ref, r_ref, v_ref))
        w = jnp.exp(w)
        ys = []
        for i in range(head):
            s = s_ref[h, i]
            sa = jnp.sum(s * a, axis=0, keepdims=True)
            s_new = s * w + sa * b + v[i:i + 1, :] * k
            so_ref[h, i] = s_new
            ys.append(jnp.sum(s_new * r, axis=0, keepdims=True))
        y_ref[h] = jnp.concatenate(ys, axis=0)


def _wkv_step(r, lw, k, v, a, b, state, layer, prev_out):
    H, N, DB = r.shape
    hb = WKV_STEP_HEADS
    assert H % hb == 0
    vec = pl.BlockSpec((hb, N, DB), lambda g: (g, 0, 0))
    mat = pl.BlockSpec((None, hb, N, N, DB), lambda g: (layer, g, 0, 0, 0))
    in_specs = [vec] * 6 + [mat]
    args = [r, lw, k, v, a, b, state]
    aliases = {}
    if prev_out is not None:
        in_specs.append(pl.BlockSpec(memory_space=pl.ANY))
        args.append(prev_out)
        aliases = {7: 1}
    return pl.pallas_call(
        functools.partial(_wkv_step_kernel, heads=hb, head=N),
        grid=(H // hb,),
        in_specs=in_specs,
        out_specs=[vec, mat],
        out_shape=[jax.ShapeDtypeStruct((H, N, DB), F32), jax.ShapeDtypeStruct(state.shape, F32)],
        input_output_aliases=aliases,
        compiler_params=pltpu.CompilerParams(dimension_semantics=("arbitrary",), vmem_limit_bytes=VMEM_LIMIT),
    )(*args)


def _rwkv_pre_kernel(*refs, head, has_vmix):
    if has_vmix:
        (k_ref, wl_ref, al_ref, v_ref, vf_ref, vl_ref, w0_ref, a0_ref, kk_ref, ka_ref, v0_ref, bd_ref,
         lw_ref, k2_ref, sa_ref, sb_ref, v2_ref) = refs
    else:
        (k_ref, wl_ref, al_ref, w0_ref, a0_ref, kk_ref, ka_ref, bd_ref, lw_ref, k2_ref, sa_ref, sb_ref) = refs
    W = MXU_DIM
    ones_bd = bd_ref[...]
    for c in range(0, k_ref.shape[1], W):
        lanes = slice(c, c + W)
        k = k_ref[:, lanes]
        w_log = -jax.nn.softplus(-(w0_ref[:, lanes] + wl_ref[:, lanes])) - 0.5
        lw_ref[:, lanes] = -jnp.exp(w_log)
        a = jax.nn.sigmoid(a0_ref[:, lanes] + al_ref[:, lanes])
        kk = k * kk_ref[:, lanes]
        kk = kk / jnp.maximum(jnp.sqrt(_head_sum(kk * kk, ones_bd)), 1e-12)
        k2_ref[:, lanes] = k * (1.0 + (a - 1.0) * ka_ref[:, lanes])
        sa_ref[:, lanes] = -kk
        sb_ref[:, lanes] = kk * a
        if has_vmix:
            v = v_ref[:, lanes]
            v2_ref[:, lanes] = v + (vf_ref[:, lanes] - v) * jax.nn.sigmoid(v0_ref[:, lanes] + vl_ref[:, lanes])


def _rwkv_pre(k, wl, al, w0, a0, k_k, k_a, head, vmix=None):
    M, D = k.shape
    W = MXU_DIM
    tm = _pick(M, (256, 128) if vmix is None else (128,))
    idx = np.arange(W)
    ones_bd = jnp.asarray((idx[:, None] // head) == (idx[None, :] // head), BF16)
    row = pl.BlockSpec((tm, D), lambda i: (i, 0))
    vec = pl.BlockSpec((1, D), lambda i: (0, 0))
    bd = pl.BlockSpec((W, W), lambda i: (0, 0))
    as_row = lambda p: p.reshape(1, D)
    if vmix is None:
        args = [k, wl, al, as_row(w0), as_row(a0), as_row(k_k), as_row(k_a), ones_bd]
        in_specs = [row] * 3 + [vec] * 4 + [bd]
        n_out = 4
    else:
        v, v_first, vl, v0 = vmix
        args = [k, wl, al, v, v_first, vl, as_row(w0), as_row(a0), as_row(k_k), as_row(k_a), as_row(v0), ones_bd]
        in_specs = [row] * 6 + [vec] * 5 + [bd]
        n_out = 5
    return pl.pallas_call(
        functools.partial(_rwkv_pre_kernel, head=head, has_vmix=vmix is not None),
        grid=(M // tm,),
        in_specs=in_specs,
        out_specs=[row] * n_out,
        out_shape=[jax.ShapeDtypeStruct((M, D), F32)] * n_out,
        compiler_params=pltpu.CompilerParams(dimension_semantics=("arbitrary",), vmem_limit_bytes=VMEM_LIMIT),
    )(*args)


def _head_sum(x, ones_bd):
    hi = x.astype(BF16)
    lo = (x - hi.astype(F32)).astype(BF16)
    return _nn(hi, ones_bd) + _nn(lo, ones_bd)


def _rwkv_out_kernel(y_ref, r_ref, k_ref, v_ref, g_ref, lw_ref, lb_ref, rk_ref, bd_ref, o_ref, *, head, eps):
    W = MXU_DIM
    ones_bd = bd_ref[...]
    inv_n = 1.0 / head
    for c in range(0, y_ref.shape[1], W):
        lanes = slice(c, c + W)
        y = y_ref[:, lanes]
        d = y - _head_sum(y, ones_bd) * inv_n
        var = _head_sum(d * d, ones_bd) * inv_n
        yn = d * lax.rsqrt(var + eps) * lw_ref[:, lanes] + lb_ref[:, lanes]
        bonus = _head_sum(r_ref[:, lanes] * k_ref[:, lanes] * rk_ref[:, lanes], ones_bd) * v_ref[:, lanes]
        o_ref[:, lanes] = ((yn + bonus) * g_ref[:, lanes]).astype(o_ref.dtype)


def _rwkv_out(y, r, k, v, g, lnx_w, lnx_b, r_k, head):
    M, D = y.shape
    W = MXU_DIM
    tm = _pick(M, (256, 128))
    idx = np.arange(W)
    ones_bd = jnp.asarray((idx[:, None] // head) == (idx[None, :] // head), BF16)
    row = pl.BlockSpec((tm, D), lambda i: (i, 0))
    vec = pl.BlockSpec((1, D), lambda i: (0, 0))
    return pl.pallas_call(
        functools.partial(_rwkv_out_kernel, head=head, eps=head * GN_EPS_PER_CHANNEL),
        grid=(M // tm,),
        in_specs=[row] * 5 + [vec] * 3 + [pl.BlockSpec((W, W), lambda i: (0, 0))],
        out_specs=row,
        out_shape=jax.ShapeDtypeStruct((M, D), BF16),
        compiler_params=pltpu.CompilerParams(dimension_semantics=("arbitrary",), vmem_limit_bytes=VMEM_LIMIT),
    )(y, r, k, v, g, lnx_w.reshape(1, D), lnx_b.reshape(1, D), r_k.reshape(1, D), ones_bd)


def _flash_kernel(qi_ref, ki_ref, flag_ref, q_ref, k_ref, v_ref, o_ref, m_sc, acc_sc, *, tq, tk, vdim, heads):
    p = pl.program_id(2)
    qi = qi_ref[p]
    ki = ki_ref[p]
    flags = flag_ref[p]
    hs = range(heads)

    @pl.when(ki == 0)
    def _():
        m_sc[...] = jnp.full_like(m_sc, -jnp.inf)
        acc_sc[...] = jnp.zeros_like(acc_sc)

    def update(masked):
        s = [_nt(q_ref[0, h], k_ref[0, h]) for h in hs]
        if masked:
            qpos = qi * tq + lax.broadcasted_iota(jnp.int32, (tq, 1), 0)
            kpos = ki * tk + lax.broadcasted_iota(jnp.int32, (1, tk), 1)
            keep = kpos <= qpos
            s = [jnp.where(keep, x, -jnp.inf) for x in s]
        m_old = [m_sc[h] for h in hs]
        m_new = [jnp.maximum(mo, jnp.max(x, axis=-1, keepdims=True)) for mo, x in zip(m_old, s)]
        pm = [jnp.exp2(x - mn).astype(BF16) for x, mn in zip(s, m_new)]
        for h in hs:
            acc_sc[h] = jnp.exp2(m_old[h] - m_new[h]) * acc_sc[h] + _nn(pm[h], v_ref[0, h])
            m_sc[h] = m_new[h]

    pl.when((flags & 2) != 0)(lambda: update(True))
    pl.when((flags & 2) == 0)(lambda: update(False))

    @pl.when((flags & 1) != 0)
    def _():
        outs = []
        for h in hs:
            acc = acc_sc[h]
            outs.append(acc[:, :vdim] / acc[:, vdim:vdim + 1])
        o_ref[0] = jnp.concatenate(outs, axis=1).astype(o_ref.dtype)


def _flash(q, k, v, vdim, tq, tk):
    B, H, T, E = q.shape
    VA = v.shape[-1]
    HS = FLASH_HEADS
    assert H % HS == 0
    pairs = [(qi, ki) for qi in range(T // tq) for ki in range(T // tk) if ki * tk <= qi * tq + tq - 1]
    n = len(pairs)
    qi_tab = np.array([p[0] for p in pairs], np.int32)
    ki_tab = np.array([p[1] for p in pairs], np.int32)
    flags = np.array([(1 if (i + 1 == n or pairs[i + 1][0] != pairs[i][0]) else 0)
                      + (2 if (ki + 1) * tk - 1 > qi * tq else 0)
                      for i, (qi, ki) in enumerate(pairs)], np.int32)
    grid_spec = pltpu.PrefetchScalarGridSpec(
        num_scalar_prefetch=3,
        grid=(B, H // HS, n),
        in_specs=[pl.BlockSpec((1, HS, tq, E), lambda b, h, p, qt, kt, ft: (b, h, qt[p], 0)),
                  pl.BlockSpec((1, HS, tk, E), lambda b, h, p, qt, kt, ft: (b, h, kt[p], 0)),
                  pl.BlockSpec((1, HS, tk, VA), lambda b, h, p, qt, kt, ft: (b, h, kt[p], 0))],
        out_specs=pl.BlockSpec((1, tq, HS * vdim), lambda b, h, p, qt, kt, ft: (b, qt[p], h)),
        scratch_shapes=[pltpu.VMEM((HS, tq, 1), F32), pltpu.VMEM((HS, tq, VA), F32)],
    )
    return pl.pallas_call(
        functools.partial(_flash_kernel, tq=tq, tk=tk, vdim=vdim, heads=HS),
        grid_spec=grid_spec,
        out_shape=jax.ShapeDtypeStruct((B, T, H * vdim), BF16),
        compiler_params=pltpu.CompilerParams(
            dimension_semantics=("arbitrary", "arbitrary", "arbitrary"), vmem_limit_bytes=VMEM_LIMIT),
    )(jnp.asarray(qi_tab), jnp.asarray(ki_tab), jnp.asarray(flags), q, k, v)


def _paged_kernel(pt_ref, *refs, heads, nope, qk_dim, n_steps, pages, have_rinv):
    del pt_ref
    c_refs, p_refs = refs[:pages], refs[pages:2 * pages]
    qa_ref, qp_ref, x_ref = refs[2 * pages:2 * pages + 3]
    rest = refs[2 * pages + 3:]
    if have_rinv:
        acc_ref, m_ref, l_ref, m_sc, l_sc, acc_sc = rest
    else:
        acc_ref, m_ref, l_ref, rinv_ref, m_sc, l_sc, acc_sc = rest
    step = pl.program_id(1)

    @pl.when(step == 0)
    def _():
        m_sc[...] = jnp.full_like(m_sc, -jnp.inf)
        l_sc[...] = jnp.zeros_like(l_sc)
        acc_sc[...] = jnp.zeros_like(acc_sc)

    c = jnp.concatenate([r[0] for r in c_refs], axis=0).astype(BF16)
    kp = jnp.concatenate([r[0] for r in p_refs], axis=1)
    tokens = c.shape[0]
    if have_rinv:
        rinv = x_ref[0, 0]
    else:
        sub = MXU_DIM
        knts = [_nt(x_ref[...], c[t:t + sub]) for t in range(0, tokens, sub)]
        parts = [jnp.sum((knt * knt).reshape(heads, nope, sub), axis=1) for knt in knts]
        ssq = jnp.concatenate(parts, axis=1)
        kss = jnp.sum(kp * kp, axis=0, keepdims=True)
        rinv = lax.rsqrt((ssq + kss) * (1.0 / qk_dim) + RMS_EPS)
        rinv_ref[0, 0] = rinv
    s = (_nt(qa_ref[0].astype(BF16), c) + _nn(qp_ref[0].astype(BF16), kp.astype(BF16))) * rinv
    m_old = m_sc[...]
    m_new = jnp.maximum(m_old, jnp.max(s, axis=-1, keepdims=True))
    alpha = jnp.exp(m_old - m_new)
    pm = jnp.exp(s - m_new)
    l_sc[...] = alpha * l_sc[...] + jnp.sum(pm, axis=-1, keepdims=True)
    acc_sc[...] = alpha * acc_sc[...] + _nn(pm.astype(BF16), c)
    m_sc[...] = m_new

    @pl.when(step == n_steps - 1)
    def _():
        acc_ref[0] = acc_sc[...]
        m_ref[0] = jnp.broadcast_to(m_sc[...], m_ref.shape[1:])
        l_ref[0] = jnp.broadcast_to(l_sc[...], l_ref.shape[1:])


def _paged_attn(page_table, cache_ckv, cache_kpe_t, qa, qp, nope, *, wnt=None, rinv=None):
    DB, n_pages = page_table.shape
    _, page, R = cache_ckv.shape
    rope = cache_kpe_t.shape[1]
    H = qa.shape[1]
    P = PAGES_PER_STEP
    assert n_pages % P == 0
    n_steps = n_pages // P
    have_rinv = rinv is not None
    page_spec = lambda shape, i: pl.BlockSpec((1,) + shape, lambda b, s, pt: (pt[b, P * s + i], 0, 0))
    rinv_spec = pl.BlockSpec((1, 1, H, P * page), lambda b, s, pt: (b, s, 0, 0))
    x_spec = rinv_spec if have_rinv else pl.BlockSpec(wnt.shape, lambda b, s, pt: (0, 0))
    stat_spec = pl.BlockSpec((1, H, LANES), lambda b, s, pt: (b, 0, 0))
    out_specs = [pl.BlockSpec((1, H, R), lambda b, s, pt: (b, 0, 0)), stat_spec, stat_spec]
    out_shape = [jax.ShapeDtypeStruct((DB, H, R), F32),
                 jax.ShapeDtypeStruct((DB, H, LANES), F32),
                 jax.ShapeDtypeStruct((DB, H, LANES), F32)]
    if not have_rinv:
        out_specs.append(rinv_spec)
        out_shape.append(jax.ShapeDtypeStruct((DB, n_steps, H, P * page), F32))
    grid_spec = pltpu.PrefetchScalarGridSpec(
        num_scalar_prefetch=1,
        grid=(DB, n_steps),
        in_specs=([page_spec((page, R), i) for i in range(P)] + [page_spec((rope, page), i) for i in range(P)]
                  + [pl.BlockSpec((1, H, R), lambda b, s, pt: (b, 0, 0)),
                     pl.BlockSpec((1, H, rope), lambda b, s, pt: (b, 0, 0)), x_spec]),
        out_specs=out_specs,
        scratch_shapes=[pltpu.VMEM((H, 1), F32), pltpu.VMEM((H, 1), F32), pltpu.VMEM((H, R), F32)],
    )
    outs = pl.pallas_call(
        functools.partial(_paged_kernel, heads=H, nope=nope, qk_dim=nope + rope, n_steps=n_steps, pages=P,
                          have_rinv=have_rinv),
        grid_spec=grid_spec,
        out_shape=out_shape,
        compiler_params=pltpu.CompilerParams(
            dimension_semantics=("arbitrary", "arbitrary"), vmem_limit_bytes=VMEM_LIMIT),
    )(page_table, *([cache_ckv] * P), *([cache_kpe_t] * P), qa, qp, rinv if have_rinv else wnt)
    acc, m, l = outs[:3]
    return acc, m[:, :, 0], l[:, :, 0], (rinv if have_rinv else outs[3])


def _pad_cols(w, n):
    return jnp.pad(w, ((0, 0), (0, n - w.shape[1])))


def _pad_rows(w, n):
    return jnp.pad(w, ((0, n - w.shape[0]), (0, 0)))


def _lora(x, w1, w2, act):
    rank = w1.shape[1]
    rp = -(-rank // LANES) * LANES
    mid = _mm(x, _pad_cols(w1, rp), act=act, out_dtype=BF16)
    return _mm(mid, _pad_rows(w2, rp))


def _rope(x, cos, sin):
    half = x.shape[-1] // 2
    x1, x2 = x[..., :half], x[..., half:]
    return jnp.concatenate([x1 * cos - x2 * sin, x2 * cos + x1 * sin], axis=-1)


def _head_rms(x, g):
    return x * lax.rsqrt(jnp.mean(x * x, axis=-1, keepdims=True) + RMS_EPS) * g


def kernel(x_prompt, x_sample, state_shift, state_wkv, state_conv, cache_ckv, cache_kpe, page_table, meta_tokens, norm_mix, norm_ffn, mu, w_rkv, w_o_a, w0, w1, w2, a0, a1, a2, v0, v1, v2, g1, g2, k_k, k_a, r_k, lnx_w, lnx_b, ffn_w_in, ffn_conv_w, ffn_conv_b, ffn_w_out, norm_kv, w_dkv, g_ckv, w_ukv, g_k, w_dq, g_q, w_uq, g_qn, w_o_b):
    B, seq, D = x_prompt.shape
    DB = x_sample.shape[0]
    assert x_sample.shape[1] == 1
    n_meta = meta_tokens.shape[0]
    depth = norm_mix.shape[0]
    n_a = mu.shape[0]
    HA, NA = r_k.shape[1], r_k.shape[2]
    assert ffn_conv_w.shape[1] == 3
    R = g_ckv.shape[0]
    rope = w_dkv.shape[1] - R
    HB = w_ukv.shape[1]
    nope = g_k.shape[0] - rope // 2
    vdim = w_ukv.shape[2] - nope
    qk = nope + rope
    T = seq + n_meta
    Tp = -(-T // ROW_ALIGN) * ROW_ALIGN
    Tpa = -(-Tp // ATTN_ALIGN) * ATTN_ALIGN
    MP = B * Tp
    M = MP + DB
    past_len = page_table.shape[1] * cache_ckv.shape[1]
    scale = qk ** -0.5

    h0 = jnp.concatenate([jnp.broadcast_to(meta_tokens[None], (B, n_meta, D)), x_prompt], axis=1)
    h0 = jnp.pad(h0, ((0, 0), (0, Tp - T), (0, 0)))
    h = jnp.concatenate([h0.reshape(MP, D), x_sample.reshape(DB, D)], axis=0)

    t_of_row = jnp.concatenate([jnp.tile(jnp.arange(Tp), B), jnp.full((DB,), past_len)])
    not_first = (t_of_row[:MP] > 0).astype(F32)[:, None]
    inv_freq = ROPE_THETA ** (-jnp.arange(0, rope, 2, dtype=F32) / rope)
    ang = t_of_row.astype(F32)[:, None] * inv_freq[None]
    cos, sin = jnp.cos(ang), jnp.sin(ang)
    last_rows = np.array([b * Tp + T - 1 for b in range(B)])

    state_wkv_t = jnp.transpose(state_wkv, (0, 2, 3, 4, 1))
    cache_kpe_t = jnp.swapaxes(cache_kpe, 1, 2)
    w_rkv3 = w_rkv.reshape(n_a * 3, D, D)
    w_uq3 = w_uq.reshape(w_uq.shape[0], w_uq.shape[1], HB * qk)
    w_ob3 = w_o_b.reshape(w_o_b.shape[0], HB * vdim, D)

    shift_p, shift_s, wkv_p, conv_p, conv_s = [], [], [], [], []
    wkv_s_t = None
    v_first = None
    kv_p = kv_s = None
    ckv = kpe = None

    for i in range(depth):
        if i < n_a:
            xn = _rmsnorm(h, norm_mix[i], F32)
            shift_p.append(xn[last_rows])
            shift_s.append(xn[MP:])
            prev = jnp.concatenate([jnp.roll(xn[:MP], 1, axis=0) * not_first, state_shift[i]], axis=0)
            dx = prev - xn
            xs = [(xn + dx * mu[i, s]).astype(BF16) for s in range(6)]
            r = _mm(xs[0], w_rkv3, layer=3 * i)
            k = _mm(xs[1], w_rkv3, layer=3 * i + 1)
            v = _mm(xs[2], w_rkv3, layer=3 * i + 2)
            wl = _lora(xs[3], w1[i], w2[i], jnp.tanh)
            al = _lora(xs[4], a1[i], a2[i], None)
            g = _lora(xs[5], g1[i], g2[i], jax.nn.sigmoid)
            if i > 0:
                vl = _lora(xs[2], v1[i - 1], v2[i - 1], None)
                log_decay, k, sa, sb, v = _rwkv_pre(k, wl, al, w0[i], a0[i], k_k[i], k_a[i], NA,
                                                    vmix=(v, v_first, vl, v0[i - 1]))
            else:
                v_first = v
                log_decay, k, sa, sb = _rwkv_pre(k, wl, al, w0[i], a0[i], k_k[i], k_a[i], NA)
            y, st_p = _wkv_chunked(r, log_decay, k, v, sa, sb, B, Tp, T, NA)
            lanes_t = lambda t: t[MP:].reshape(DB, HA, NA).transpose(1, 2, 0)
            y_s, wkv_s_t = _wkv_step(*(lanes_t(t) for t in (r, log_decay, k, v, sa, sb)), state_wkv_t, i, wkv_s_t)
            y = lax.dynamic_update_slice(y, y_s.transpose(2, 0, 1).reshape(DB, D), (MP, 0))
            wkv_p.append(st_p)
            gated = _rwkv_out(y, r, k, v, g, lnx_w[i], lnx_b[i], r_k[i], NA)
            h = _mm(gated, w_o_a, layer=i, residual=h)
        else:
            j = i - n_a
            xn = _rmsnorm(h, norm_mix[i], BF16)
            cq = _rmsnorm(_mm(xn, w_dq, layer=j), g_q[j], BF16)
            q = _mm(cq, w_uq3, layer=j).reshape(M, HB, qk)
            q = jnp.concatenate([q[..., :nope], _rope(q[..., nope:], cos[:, None], sin[:, None])], axis=-1)
            gq = g_qn[j]
            q = _head_rms(q, jnp.concatenate([gq[:nope], gq[nope:], gq[nope:]])) * scale
            q_p, q_s = (q[:MP] * LOG2E).reshape(B, Tp, HB, qk), q[MP:]
            q_p = jnp.pad(q_p.transpose(0, 2, 1, 3), ((0, 0), (0, 0), (0, Tpa - Tp), (0, 0))).astype(BF16)
            o_p = _flash(q_p, kv_p[0], kv_p[1], vdim, tq=Tpa // FLASH_BLOCKS, tk=Tpa // FLASH_BLOCKS)[:, :Tp]
            qn = q_s[..., :nope] * g_k[:nope]
            qp = q_s[..., nope:] * jnp.concatenate([g_k[nope:], g_k[nope:]])
            qa = _bmm(qn.transpose(1, 0, 2), kv_s['wnt3']).transpose(1, 0, 2)
            acc, m, l, kv_s['rinv_cache'] = _paged_attn(
                page_table, cache_ckv, cache_kpe_t, qa, qp, nope, wnt=kv_s['wnt'], rinv=kv_s['rinv_cache'])
            c_new, kp_new = kv_s['ckv'], kv_s['kpe']
            s_new = kv_s['rinv'] * (jnp.sum(qa * c_new[:, None, :], axis=-1) + jnp.sum(qp * kp_new[:, None, :], axis=-1))
            m_f = jnp.maximum(m, s_new)
            alpha = jnp.exp(m - m_f)
            pn = jnp.exp(s_new - m_f)
            l_f = l * alpha + pn
            ctx = (acc * alpha[..., None] + pn[..., None] * c_new[:, None, :]) / l_f[..., None]
            o_s = _bmm(ctx.transpose(1, 0, 2), kv_s['wv']).transpose(1, 0, 2).reshape(DB, HB * vdim)
            attn = jnp.concatenate([o_p.reshape(MP, HB * vdim), o_s.astype(BF16)], axis=0)
            h = _mm(attn, w_ob3, layer=j, residual=h)

        xn = _rmsnorm(h, norm_ffn[i], BF16)
        gated, c_tail, c_s = _ffn_in(xn, ffn_w_in, i, ffn_conv_w[i], ffn_conv_b[i], state_conv[i], B, Tp, T)
        conv_p.append(c_tail)
        conv_s.append(jnp.stack([state_conv[i][:, 1], c_s], axis=1))
        h = _mm(gated, ffn_w_out, layer=i, residual=h)

        if i == n_a - 1:
            xk = _rmsnorm(h, norm_kv, BF16)
            ckv = _rmsnorm(_mm(xk, w_dkv[:, :R]), g_ckv, F32)
            kpe = _rope(_mm(xk, w_dkv[:, R:]), cos, sin)
            kv = _mm(ckv.astype(BF16), w_ukv.reshape(R, HB * (nope + vdim))).reshape(M, HB, nope + vdim)
            k_nope = kv[..., :nope]
            ssq = jnp.sum(k_nope * k_nope, axis=-1) + jnp.sum(kpe * kpe, axis=-1, keepdims=True)
            rinv = lax.rsqrt(ssq / qk + RMS_EPS)
            gk_full = jnp.concatenate([g_k[:nope], g_k[nope:], g_k[nope:]])
            k_full = jnp.concatenate([k_nope, jnp.broadcast_to(kpe[:, None, :], (M, HB, rope))], axis=-1)
            k_full = k_full * rinv[..., None] * gk_full
            v_aug = jnp.concatenate([kv[..., nope:], jnp.ones((M, HB, 1), F32), jnp.zeros((M, HB, vdim - 1), F32)],
                                    axis=-1)
            to_heads = lambda t: jnp.pad(t[:MP].reshape(B, Tp, HB, -1).transpose(0, 2, 1, 3),
                                         ((0, 0), (0, 0), (0, Tpa - Tp), (0, 0))).astype(BF16)
            kv_p = (to_heads(k_full), to_heads(v_aug))
            wnt3 = w_ukv[:, :, :nope].transpose(1, 2, 0)
            kv_s = dict(ckv=ckv[MP:], kpe=kpe[MP:], rinv=rinv[MP:], wnt3=wnt3, rinv_cache=None,
                        wnt=wnt3.reshape(HB * nope, R).astype(BF16),
                        wv=w_ukv[:, :, nope:].transpose(1, 0, 2))

    wkv_s = jnp.transpose(wkv_s_t, (0, 4, 1, 2, 3))
    return (h[:MP].reshape(B, Tp, D)[:, n_meta:T], h[MP:].reshape(DB, 1, D),
            jnp.stack(shift_p), jnp.stack(wkv_p), jnp.stack(conv_p),
            ckv[:MP].reshape(B, Tp, R)[:, :T], kpe[:MP].reshape(B, Tp, rope)[:, :T],
            jnp.stack(shift_s), wkv_s, jnp.stack(conv_s),
            ckv[MP:].reshape(DB, 1, R), kpe[MP:].reshape(DB, 1, rope))
```

```python
import functools

import numpy as np
import jax
import jax.numpy as jnp
from jax import lax
from jax.experimental import pallas as pl
from jax.experimental.pallas import tpu as pltpu

F32 = jnp.float32
BF16 = jnp.bfloat16

RMS_EPS = 1e-6
ROPE_THETA = 10000.0
GN_EPS_PER_CHANNEL = 1e-5
LOG2E = 1.4426950408889634
LANES = 128
SUBLANES = 8
MXU_DIM = 256
WKV_CHUNK = 64
WKV_UNITS = 4
WKV_STEP_HEADS = 2
ROW_ALIGN = 64
ATTN_ALIGN = 128
FLASH_BLOCKS = 3
FLASH_HEADS = 2
PAGES_PER_STEP = 8
VMEM_LIMIT = 56 * 1024 * 1024


def _nt(a, b):
    return lax.dot_general(a, b, (((1,), (1,)), ((), ())), preferred_element_type=F32)


def _tn(a, b):
    return lax.dot_general(a, b, (((0,), (0,)), ((), ())), preferred_element_type=F32)


def _nn(a, b):
    return jnp.dot(a, b, preferred_element_type=F32)


def _pick(n, candidates):
    for c in candidates:
        if n % c == 0:
            return c
    return n


def _mm_kernel(x_ref, w_ref, *rest, act, has_res):
    if has_res:
        r_ref, o_ref, wb_ref = rest
    else:
        o_ref, wb_ref = rest

    @pl.when(pl.program_id(1) == 0)
    def _():
        wb_ref[...] = w_ref[...].astype(BF16)

    acc = _nn(x_ref[...].astype(BF16), wb_ref[...])
    if act is not None:
        acc = act(acc)
    if has_res:
        acc = acc + r_ref[...]
    o_ref[...] = acc.astype(o_ref.dtype)


def _mm(x, w, *, layer=None, act=None, residual=None, out_dtype=F32):
    M, K = x.shape
    N = w.shape[-1]
    tm = _pick(M, (768, 512, 384, 256, 128))
    if K > 4096:
        tm = _pick(M, (384, 256, 128))
    tn = _pick(N, (1024, 512, 256, 128)) if K <= 2048 else _pick(N, (512, 256, 128))
    grid = (N // tn, M // tm)
    if layer is None:
        w_spec = pl.BlockSpec((K, tn), lambda j, i: (0, j))
    else:
        w_spec = pl.BlockSpec((None, K, tn), lambda j, i: (layer, 0, j))
    in_specs = [pl.BlockSpec((tm, K), lambda j, i: (i, 0)), w_spec]
    args = [x, w]
    if residual is not None:
        in_specs.append(pl.BlockSpec((tm, tn), lambda j, i: (i, j)))
        args.append(residual)
    return pl.pallas_call(
        functools.partial(_mm_kernel, act=act, has_res=residual is not None),
        grid=grid,
        in_specs=in_specs,
        out_specs=pl.BlockSpec((tm, tn), lambda j, i: (i, j)),
        out_shape=jax.ShapeDtypeStruct((M, N), out_dtype),
        scratch_shapes=[pltpu.VMEM((K, tn), BF16)],
        compiler_params=pltpu.CompilerParams(
            dimension_semantics=("arbitrary", "arbitrary"), vmem_limit_bytes=VMEM_LIMIT),
    )(*args)


def _bmm_kernel(x_ref, w_ref, o_ref):
    o_ref[0] = _nn(x_ref[0].astype(BF16), w_ref[0].astype(BF16))


def _bmm(x, w):
    G, M, K = x.shape
    N = w.shape[2]
    return pl.pallas_call(
        _bmm_kernel,
        grid=(G,),
        in_specs=[pl.BlockSpec((1, M, K), lambda g: (g, 0, 0)),
                  pl.BlockSpec((1, K, N), lambda g: (g, 0, 0))],
        out_specs=pl.BlockSpec((1, M, N), lambda g: (g, 0, 0)),
        out_shape=jax.ShapeDtypeStruct((G, M, N), F32),
    )(x, w)


def _rms_kernel(x_ref, g_ref, o_ref):
    x = x_ref[...].astype(F32)
    y = x * lax.rsqrt(jnp.mean(x * x, axis=-1, keepdims=True) + RMS_EPS)
    o_ref[...] = (y * g_ref[...]).astype(o_ref.dtype)


def _rmsnorm(x, g, out_dtype):
    M, D = x.shape
    tm = _pick(M, (768, 512, 384, 256, 128))
    return pl.pallas_call(
        _rms_kernel,
        grid=(M // tm,),
        in_specs=[pl.BlockSpec((tm, D), lambda i: (i, 0)),
                  pl.BlockSpec((1, D), lambda i: (0, 0))],
        out_specs=pl.BlockSpec((tm, D), lambda i: (i, 0)),
        out_shape=jax.ShapeDtypeStruct((M, D), out_dtype),
    )(x, g.reshape(1, D).astype(F32))


def _gate(c, p1, p2, z, cw_ref, cb_ref):
    conv = cb_ref[...] + p2 * cw_ref[0:1, :] + p1 * cw_ref[1:2, :] + c * cw_ref[2:3, :]
    return (jax.nn.silu(conv) * z).astype(BF16)


def _ffn_in_prompt_kernel(x_ref, wc_ref, wz_ref, cw_ref, cb_ref, g_ref, tail_ref, wcb_ref, wzb_ref, cs_ref,
                          *, tm, seq_rows, n_seq, tails):
    i = pl.program_id(1)
    S = SUBLANES

    @pl.when(i == 0)
    def _():
        wcb_ref[...] = wc_ref[...].astype(BF16)
        wzb_ref[...] = wz_ref[...].astype(BF16)
        cs_ref[0:S, :] = jnp.zeros((S, cs_ref.shape[1]), F32)

    x = x_ref[...]
    c = _nn(x, wcb_ref[...])
    z = _nn(x, wzb_ref[...])
    cs_ref[S:S + tm, :] = c
    p1 = cs_ref[S - 1:S - 1 + tm, :]
    p2 = cs_ref[S - 2:S - 2 + tm, :]
    row = i * tm + lax.broadcasted_iota(jnp.int32, (tm, 1), 0)
    t = row
    for b in range(1, n_seq):
        t = jnp.where(row >= b * seq_rows, row - b * seq_rows, t)
    p1 = jnp.where(t >= 1, p1, 0.0)
    p2 = jnp.where(t >= 2, p2, 0.0)
    g_ref[...] = _gate(c, p1, p2, z, cw_ref, cb_ref)
    cs_ref[0:S, :] = cs_ref[tm:tm + S, :]
    for b, (tile, off) in enumerate(tails):
        @pl.when(i == tile)
        def _(b=b, off=off):
            tail_ref[b] = c[off:off + S]


def _ffn_in_sample_kernel(x_ref, wc_ref, wz_ref, cw_ref, cb_ref, b0_ref, b1_ref, alias_ref, g_ref, c_ref):
    del alias_ref
    x = x_ref[...]
    c = _nn(x, wc_ref[...].astype(BF16))
    z = _nn(x, wz_ref[...].astype(BF16))
    c_ref[...] = c
    g_ref[...] = _gate(c, b1_ref[...], b0_ref[...], z, cw_ref, cb_ref)


def _ffn_in(xn, w_in, layer, conv_w, conv_b, buf, n_seq, seq_rows, t_valid):
    M, K = xn.shape
    F = conv_b.shape[0]
    MP = n_seq * seq_rows
    DB = M - MP
    S = SUBLANES
    tn = _pick(F, (512, 256, 128))
    nj = F // tn
    tm = _pick(MP, (640, 512, 384, 256, 128))
    cw = conv_w.astype(F32)
    cb = conv_b.reshape(1, F).astype(F32)
    t0 = t_valid - 2
    assert t0 % S <= S - 2 and MP % DB == 0
    tails = []
    for b in range(n_seq):
        r0 = b * seq_rows + (t0 // S) * S
        assert r0 // tm == (r0 + S - 1) // tm
        tails.append((r0 // tm, r0 % tm))
    wspec = lambda off: pl.BlockSpec((None, K, tn), lambda j, i: (layer, 0, j + off))
    cspec = lambda rows: pl.BlockSpec((rows, tn), lambda j, i: (0, j))
    gated, tail = pl.pallas_call(
        functools.partial(_ffn_in_prompt_kernel, tm=tm, seq_rows=seq_rows, n_seq=n_seq, tails=tuple(tails)),
        grid=(nj, MP // tm),
        in_specs=[pl.BlockSpec((tm, K), lambda j, i: (i, 0)), wspec(0), wspec(nj), cspec(3), cspec(1)],
        out_specs=[pl.BlockSpec((tm, tn), lambda j, i: (i, j)),
                   pl.BlockSpec((n_seq, S, tn), lambda j, i: (0, 0, j))],
        out_shape=[jax.ShapeDtypeStruct((M, F), BF16), jax.ShapeDtypeStruct((n_seq, S, F), F32)],
        scratch_shapes=[pltpu.VMEM((K, tn), BF16), pltpu.VMEM((K, tn), BF16), pltpu.VMEM((tm + S, tn), F32)],
        compiler_params=pltpu.CompilerParams(
            dimension_semantics=("arbitrary", "arbitrary"), vmem_limit_bytes=VMEM_LIMIT),
    )(xn, w_in, w_in, cw, cb)
    wspec1 = lambda off: pl.BlockSpec((None, K, tn), lambda j: (layer, 0, j + off))
    cspec1 = lambda rows: pl.BlockSpec((rows, tn), lambda j: (0, j))
    gated, c_s = pl.pallas_call(
        _ffn_in_sample_kernel,
        grid=(nj,),
        in_specs=[pl.BlockSpec((DB, K), lambda j: (MP // DB, 0)), wspec1(0), wspec1(nj), cspec1(3), cspec1(1),
                  cspec1(DB), cspec1(DB), pl.BlockSpec(memory_space=pl.ANY)],
        out_specs=[pl.BlockSpec((DB, tn), lambda j: (MP // DB, j)), cspec1(DB)],
        out_shape=[jax.ShapeDtypeStruct((M, F), BF16), jax.ShapeDtypeStruct((DB, F), F32)],
        input_output_aliases={7: 0},
        compiler_params=pltpu.CompilerParams(dimension_semantics=("arbitrary",), vmem_limit_bytes=VMEM_LIMIT),
    )(xn, w_in, w_in, cw, cb, buf[:, 0], buf[:, 1], gated)
    off = t0 % S
    return gated, tail[:, off:off + 2], c_s


def _wkv_units(r, lw, k, v, a, b, sts, consts, *, chunk, groups, levels):
    bd, code, tri, eye = consts
    W = MXU_DIM
    units = len(sts)
    hi = lw.astype(BF16)
    rem = lw - hi.astype(F32)
    mid = rem.astype(BF16)
    lo = (rem - mid.astype(F32)).astype(BF16)
    lc = _nn(tri, hi) + _nn(tri, mid) + _nn(tri, lo)
    lc_end = lc[chunk - 1:chunk, :]
    e_neg = jnp.exp(-lc)
    e_end = jnp.exp(lc_end - lc)
    d_end = jnp.exp(lc_end)

    def cut(x):
        return [x[:, u * W:(u + 1) * W] for u in range(units)]

    def each(f, *lists):
        return [f(*xs) for xs in zip(*lists)]

    def stack(x):
        return jnp.concatenate([x] * groups, axis=0) * bd

    def stack_b(x):
        return stack(x).astype(BF16)

    bf = lambda x: x.astype(BF16)
    rt_f = each(stack, cut(r * jnp.exp(lc)))
    rt_s = each(bf, rt_f)
    at_s = each(stack_b, cut(a * jnp.exp(lc - lw)))
    kt_s = each(stack_b, cut(k * e_neg))
    bt_s = each(stack_b, cut(b * e_neg))
    kh_s = each(stack_b, cut(k * e_end))
    bh_s = each(stack_b, cut(b * e_end))
    v_s = each(stack_b, cut(v))
    d_end = cut(d_end)

    strict = (code >= 0) & (code < levels)
    incl = code >= 0
    m_ab = each(_nt, at_s, bt_s)
    m_ak = each(lambda x, y: bf(jnp.where(strict, _nt(x, y), 0.0)), at_s, kt_s)
    n_rb = each(lambda x, y: bf(jnp.where(incl, _nt(x, y), 0.0)), rt_s, bt_s)
    n_rk = each(lambda x, y: bf(jnp.where(incl, _nt(x, y), 0.0)), rt_s, kt_s)
    mv_b = each(lambda x, y: bf(_nn(x, y)), m_ak, v_s)

    inv = each(lambda m: eye + jnp.where(code == 0, m, 0.0), m_ab)
    for lev in range(1, levels):
        inv_b = each(bf, inv)
        off = each(lambda m: bf(jnp.where(code == lev, m, 0.0)), m_ab)
        mid_b = each(lambda o, t: bf(_nn(o, t)), off, inv_b)
        inv = each(lambda t, tb, x: t + _nn(tb, x), inv, inv_b, mid_b)
    inv_b = each(bf, inv)

    p_b = each(lambda t, x: bf(_nn(t, x)), inv_b, at_s)
    q_b = each(lambda t, x: bf(_nn(t, x)), inv_b, mv_b)

    g_b = each(lambda d, x, y: bf(eye * d + _tn(x, y)), d_end, bh_s, p_b)
    f_mat = each(lambda x, y, z, w: _tn(x, y) + _tn(z, w), bh_s, q_b, kh_s, v_s)
    ry_b = each(lambda x, n, p: bf(x + _nn(n, p)), rt_f, n_rb, p_b)
    y_0 = each(lambda n, q, m, w: _nn(n, q) + _nn(m, w), n_rb, q_b, n_rk, v_s)

    st_b = each(bf, sts)
    y_s = each(lambda x, s, y0: _nn(x, s) + y0, ry_b, st_b, y_0)
    new_sts = each(lambda g, s, f: _nn(g, s) + f, g_b, st_b, f_mat)

    def unstack(x):
        y = x[0:chunk]
        for h in range(1, groups):
            y = y + x[h * chunk:(h + 1) * chunk]
        return y

    return jnp.concatenate(each(unstack, y_s), axis=1), new_sts


def _wkv_chunk_kernel(r_ref, lw_ref, k_ref, v_ref, a_ref, b_ref, bd_ref, code_ref, tri_ref, eye_ref,
                      y_ref, s_ref, st_ref, *, chunk, groups, units, t_valid, n_chunks, levels):
    c = pl.program_id(1)

    @pl.when(c == 0)
    def _():
        st_ref[...] = jnp.zeros_like(st_ref)

    row = c * chunk + lax.broadcasted_iota(jnp.int32, (chunk, 1), 0)
    valid = row < t_valid
    consts = (bd_ref[...], code_ref[...], tri_ref[...], eye_ref[...])
    ins = (jnp.where(valid, ref[...], 0.0) for ref in (r_ref, lw_ref, k_ref, v_ref, a_ref, b_ref))
    y, sts = _wkv_units(*ins, [st_ref[u] for u in range(units)], consts, chunk=chunk, groups=groups, levels=levels)
    y_ref[...] = y
    for u in range(units):
        st_ref[u] = sts[u]

    @pl.when(c == n_chunks - 1)
    def _():
        s_ref[0] = st_ref[...]


def _wkv_chunked(r, lw, k, v, a, b, n_seq, seq_rows, t_valid, head):
    M, D = r.shape
    L = WKV_CHUNK
    W = MXU_DIM
    U = WKV_UNITS
    G = W // head
    nu = D // (W * U)
    nc = seq_rows // L
    assert seq_rows % L == 0 and D % (W * U) == 0 and G * L == W
    levels = int(np.log2(L))
    assert 2 ** levels == L
    idx = np.arange(W)
    same = (idx[:, None] // L) == (idx[None, :] // L)
    bd = same.astype(np.float32)
    diff = idx[:, None] ^ idx[None, :]
    code = np.floor(np.log2(np.maximum(diff, 1))).astype(np.int32)
    code = np.where(same & (idx[None, :] < idx[:, None]), code, -1)
    code = np.where(idx[None, :] == idx[:, None], levels, code).astype(np.int32)
    tri = jnp.asarray(np.tril(np.ones((L, L), np.float32)), BF16)
    eye = np.eye(W, dtype=np.float32)

    seq = pl.BlockSpec((L, W * U), lambda p, c: ((p // nu) * nc + c, p % nu))
    const = lambda shape: pl.BlockSpec(shape, lambda p, c: (0, 0))
    y, st = pl.pallas_call(
        functools.partial(_wkv_chunk_kernel, chunk=L, groups=G, units=U, t_valid=t_valid, n_chunks=nc,
                          levels=levels),
        grid=(n_seq * nu, nc),
        in_specs=[seq] * 6 + [const((W, W))] * 2 + [const((L, L)), const((W, W))],
        out_specs=[seq, pl.BlockSpec((1, U, W, W), lambda p, c: (p, 0, 0, 0))],
        out_shape=[jax.ShapeDtypeStruct((M, D), F32),
                   jax.ShapeDtypeStruct((n_seq * nu, U, W, W), F32)],
        scratch_shapes=[pltpu.VMEM((U, W, W), F32)],
        compiler_params=pltpu.CompilerParams(dimension_semantics=("arbitrary", "arbitrary")),
    )(r, lw, k, v, a, b, jnp.asarray(bd), jnp.asarray(code), tri, jnp.asarray(eye))
    ng = nu * U
    st = st.reshape(n_seq, ng, G, head, G, head)
    st = jnp.stack([st[:, :, h, :, h, :] for h in range(G)], axis=2)
    return y, jnp.swapaxes(st, -1, -2).reshape(n_seq, ng * G, head, head)


def _wkv_step_kernel(r_ref, lw_ref, k_ref, v_ref, a_ref, b_ref, s_ref, *rest, heads, head):
    y_ref, so_ref = rest[-2:]
    for h in range(heads):
        w, a, b, k, r, v = (ref[h] for ref in (lw_ref, a_ref, b_ref, k_ref, r_ref, v_ref))
        w = jnp.exp(w)
        ys = []
        for i in range(head):
            s = s_ref[h, i]
            sa = jnp.sum(s * a, axis=0, keepdims=True)
            s_new = s * w + sa * b + v[i:i + 1, :] * k
            so_ref[h, i] = s_new
            ys.append(jnp.sum(s_new * r, axis=0, keepdims=True))
        y_ref[h] = jnp.concatenate(ys, axis=0)


def _wkv_step(r, lw, k, v, a, b, state, layer, prev_out):
    H, N, DB = r.shape
    hb = WKV_STEP_HEADS
    assert H % hb == 0
    vec = pl.BlockSpec((hb, N, DB), lambda g: (g, 0, 0))
    mat = pl.BlockSpec((None, hb, N, N, DB), lambda g: (layer, g, 0, 0, 0))
    in_specs = [vec] * 6 + [mat]
    args = [r, lw, k, v, a, b, state]
    aliases = {}
    if prev_out is not None:
        in_specs.append(pl.BlockSpec(memory_space=pl.ANY))
        args.append(prev_out)
        aliases = {7: 1}
    return pl.pallas_call(
        functools.partial(_wkv_step_kernel, heads=hb, head=N),
        grid=(H // hb,),
        in_specs=in_specs,
        out_specs=[vec, mat],
        out_shape=[jax.ShapeDtypeStruct((H, N, DB), F32), jax.ShapeDtypeStruct(state.shape, F32)],
        input_output_aliases=aliases,
        compiler_params=pltpu.CompilerParams(dimension_semantics=("arbitrary",), vmem_limit_bytes=VMEM_LIMIT),
    )(*args)


def _rwkv_pre_kernel(*refs, head, has_vmix):
    if has_vmix:
        (k_ref, wl_ref, al_ref, v_ref, vf_ref, vl_ref, w0_ref, a0_ref, kk_ref, ka_ref, v0_ref, bd_ref,
         lw_ref, k2_ref, sa_ref, sb_ref, v2_ref) = refs
    else:
        (k_ref, wl_ref, al_ref, w0_ref, a0_ref, kk_ref, ka_ref, bd_ref, lw_ref, k2_ref, sa_ref, sb_ref) = refs
    W = MXU_DIM
    ones_bd = bd_ref[...]
    for c in range(0, k_ref.shape[1], W):
        lanes = slice(c, c + W)
        k = k_ref[:, lanes]
        w_log = -jax.nn.softplus(-(w0_ref[:, lanes] + wl_ref[:, lanes])) - 0.5
        lw_ref[:, lanes] = -jnp.exp(w_log)
        a = jax.nn.sigmoid(a0_ref[:, lanes] + al_ref[:, lanes])
        kk = k * kk_ref[:, lanes]
        kk = kk / jnp.maximum(jnp.sqrt(_head_sum(kk * kk, ones_bd)), 1e-12)
        k2_ref[:, lanes] = k * (1.0 + (a - 1.0) * ka_ref[:, lanes])
        sa_ref[:, lanes] = -kk
        sb_ref[:, lanes] = kk * a
        if has_vmix:
            v = v_ref[:, lanes]
            v2_ref[:, lanes] = v + (vf_ref[:, lanes] - v) * jax.nn.sigmoid(v0_ref[:, lanes] + vl_ref[:, lanes])


def _rwkv_pre(k, wl, al, w0, a0, k_k, k_a, head, vmix=None):
    M, D = k.shape
    W = MXU_DIM
    tm = _pick(M, (256, 128) if vmix is None else (128,))
    idx = np.arange(W)
    ones_bd = jnp.asarray((idx[:, None] // head) == (idx[None, :] // head), BF16)
    row = pl.BlockSpec((tm, D), lambda i: (i, 0))
    vec = pl.BlockSpec((1, D), lambda i: (0, 0))
    bd = pl.BlockSpec((W, W), lambda i: (0, 0))
    as_row = lambda p: p.reshape(1, D)
    if vmix is None:
        args = [k, wl, al, as_row(w0), as_row(a0), as_row(k_k), as_row(k_a), ones_bd]
        in_specs = [row] * 3 + [vec] * 4 + [bd]
        n_out = 4
    else:
        v, v_first, vl, v0 = vmix
        args = [k, wl, al, v, v_first, vl, as_row(w0), as_row(a0), as_row(k_k), as_row(k_a), as_row(v0), ones_bd]
        in_specs = [row] * 6 + [vec] * 5 + [bd]
        n_out = 5
    return pl.pallas_call(
        functools.partial(_rwkv_pre_kernel, head=head, has_vmix=vmix is not None),
        grid=(M // tm,),
        in_specs=in_specs,
        out_specs=[row] * n_out,
        out_shape=[jax.ShapeDtypeStruct((M, D), F32)] * n_out,
        compiler_params=pltpu.CompilerParams(dimension_semantics=("arbitrary",), vmem_limit_bytes=VMEM_LIMIT),
    )(*args)


def _head_sum(x, ones_bd):
    hi = x.astype(BF16)
    lo = (x - hi.astype(F32)).astype(BF16)
    return _nn(hi, ones_bd) + _nn(lo, ones_bd)


def _rwkv_out_kernel(y_ref, r_ref, k_ref, v_ref, g_ref, lw_ref, lb_ref, rk_ref, bd_ref, o_ref, *, head, eps):
    W = MXU_DIM
    ones_bd = bd_ref[...]
    inv_n = 1.0 / head
    for c in range(0, y_ref.shape[1], W):
        lanes = slice(c, c + W)
        y = y_ref[:, lanes]
        d = y - _head_sum(y, ones_bd) * inv_n
        var = _head_sum(d * d, ones_bd) * inv_n
        yn = d * lax.rsqrt(var + eps) * lw_ref[:, lanes] + lb_ref[:, lanes]
        bonus = _head_sum(r_ref[:, lanes] * k_ref[:, lanes] * rk_ref[:, lanes], ones_bd) * v_ref[:, lanes]
        o_ref[:, lanes] = ((yn + bonus) * g_ref[:, lanes]).astype(o_ref.dtype)


def _rwkv_out(y, r, k, v, g, lnx_w, lnx_b, r_k, head):
    M, D = y.shape
    W = MXU_DIM
    tm = _pick(M, (256, 128))
    idx = np.arange(W)
    ones_bd = jnp.asarray((idx[:, None] // head) == (idx[None, :] // head), BF16)
    row = pl.BlockSpec((tm, D), lambda i: (i, 0))
    vec = pl.BlockSpec((1, D), lambda i: (0, 0))
    return pl.pallas_call(
        functools.partial(_rwkv_out_kernel, head=head, eps=head * GN_EPS_PER_CHANNEL),
        grid=(M // tm,),
        in_specs=[row] * 5 + [vec] * 3 + [pl.BlockSpec((W, W), lambda i: (0, 0))],
        out_specs=row,
        out_shape=jax.ShapeDtypeStruct((M, D), BF16),
        compiler_params=pltpu.CompilerParams(dimension_semantics=("arbitrary",), vmem_limit_bytes=VMEM_LIMIT),
    )(y, r, k, v, g, lnx_w.reshape(1, D), lnx_b.reshape(1, D), r_k.reshape(1, D), ones_bd)


def _split_dot(x, w):
    hi = x.astype(BF16)
    lo = (x - hi.astype(F32)).astype(BF16)
    return _nn(hi, w) + _nn(lo, w)


def _qk_prep_kernel(*refs, qk_dim, rope):
    if rope:
        (n_ref, x1_ref, x2_ref, c_ref, s_ref, rs_ref, gn_ref, g1_ref, g2_ref, sn_ref, sr_ref, snt_ref, srt_ref,
         on_ref, o1_ref, o2_ref, ri_ref) = refs
        x1, x2, c, s = x1_ref[...], x2_ref[...], c_ref[...], s_ref[...]
        r1 = x1 * c - x2 * s
        r2 = x2 * c + x1 * s
    else:
        (n_ref, x1_ref, x2_ref, rs_ref, gn_ref, g1_ref, g2_ref, sn_ref, sr_ref, snt_ref, srt_ref,
         on_ref, o1_ref, o2_ref, ri_ref) = refs
        r1, r2 = x1_ref[...], x2_ref[...]
    n = n_ref[...]
    ssq = _split_dot(n * n, sn_ref[...]) + _split_dot(r1 * r1 + r2 * r2, sr_ref[...])
    rinv = lax.rsqrt(ssq * (1.0 / qk_dim) + RMS_EPS)
    ri_ref[...] = rinv
    rinv = rinv * rs_ref[...]
    rn = _split_dot(rinv, snt_ref[...])
    rr = _split_dot(rinv, srt_ref[...])
    on_ref[...] = (n * rn * gn_ref[...]).astype(on_ref.dtype)
    o1_ref[...] = (r1 * rr * g1_ref[...]).astype(o1_ref.dtype)
    o2_ref[...] = (r2 * rr * g2_ref[...]).astype(o2_ref.dtype)


def _qk_prep(src_n, col_n, src_1, col_1, src_2, col_2, heads, nope, half, gain, row_scale, cos_t=None, sin_t=None):
    M = src_n.shape[0]
    wn, wr = heads * nope, heads * half
    tm = _pick(M, (256, 128))
    seg = lambda width, per: jnp.asarray(
        (np.arange(width)[:, None] // per) == np.arange(LANES)[None, :], BF16)
    sn, sr = seg(wn, nope), seg(wr, half)
    gn = jnp.tile(gain[:nope], heads).reshape(1, wn)
    gr = jnp.tile(gain[nope:], heads).reshape(1, wr)
    rows = lambda width, col: pl.BlockSpec((tm, width), lambda i: (i, col))
    full = lambda shape: pl.BlockSpec(shape, lambda i: (0, 0))
    rope = cos_t is not None
    args = [src_n, src_1, src_2] + ([cos_t, sin_t] if rope else []) + [row_scale, gn, gr, gr, sn, sr, sn.T, sr.T]
    in_specs = ([rows(wn, col_n), rows(wr, col_1), rows(wr, col_2)] + ([rows(wr, 0)] * 2 if rope else [])
                + [rows(1, 0), full((1, wn)), full((1, wr)), full((1, wr)),
                   full((wn, LANES)), full((wr, LANES)), full((LANES, wn)), full((LANES, wr))])
    return pl.pallas_call(
        functools.partial(_qk_prep_kernel, qk_dim=nope + 2 * half, rope=rope),
        grid=(M // tm,),
        in_specs=in_specs,
        out_specs=[rows(wn, 0), rows(wr, 0), rows(wr, 0), rows(LANES, 0)],
        out_shape=[jax.ShapeDtypeStruct((M, wn), BF16), jax.ShapeDtypeStruct((M, wr), BF16),
                   jax.ShapeDtypeStruct((M, wr), BF16), jax.ShapeDtypeStruct((M, LANES), F32)],
        compiler_params=pltpu.CompilerParams(dimension_semantics=("arbitrary",), vmem_limit_bytes=VMEM_LIMIT),
    )(*args)


def _flash_kernel(qi_ref, ki_ref, flag_ref, q_ref, k_ref, v_ref, o_ref, m_sc, acc_sc, *, tq, tk, vdim, heads):
    p = pl.program_id(2)
    qi = qi_ref[p]
    ki = ki_ref[p]
    flags = flag_ref[p]
    hs = range(heads)

    @pl.when(ki == 0)
    def _():
        m_sc[...] = jnp.full_like(m_sc, -jnp.inf)
        acc_sc[...] = jnp.zeros_like(acc_sc)

    def update(masked):
        s = [_nt(q_ref[0, h], k_ref[0, h]) for h in hs]
        if masked:
            qpos = qi * tq + lax.broadcasted_iota(jnp.int32, (tq, 1), 0)
            kpos = ki * tk + lax.broadcasted_iota(jnp.int32, (1, tk), 1)
            keep = kpos <= qpos
            s = [jnp.where(keep, x, -jnp.inf) for x in s]
        m_old = [m_sc[h] for h in hs]
        m_new = [jnp.maximum(mo, jnp.max(x, axis=-1, keepdims=True)) for mo, x in zip(m_old, s)]
        pm = [jnp.exp2(x - mn).astype(BF16) for x, mn in zip(s, m_new)]
        for h in hs:
            acc_sc[h] = jnp.exp2(m_old[h] - m_new[h]) * acc_sc[h] + _nn(pm[h], v_ref[0, h])
            m_sc[h] = m_new[h]

    pl.when((flags & 2) != 0)(lambda: update(True))
    pl.when((flags & 2) == 0)(lambda: update(False))

    @pl.when((flags & 1) != 0)
    def _():
        outs = []
        for h in hs:
            acc = acc_sc[h]
            outs.append(acc[:, :vdim] / acc[:, vdim:vdim + 1])
        o_ref[0] = jnp.concatenate(outs, axis=1).astype(o_ref.dtype)


def _flash(q, k, v, vdim, tq, tk):
    B, H, T, E = q.shape
    VA = v.shape[-1]
    HS = FLASH_HEADS
    assert H % HS == 0
    pairs = [(qi, ki) for qi in range(T // tq) for ki in range(T // tk) if ki * tk <= qi * tq + tq - 1]
    n = len(pairs)
    qi_tab = np.array([p[0] for p in pairs], np.int32)
    ki_tab = np.array([p[1] for p in pairs], np.int32)
    flags = np.array([(1 if (i + 1 == n or pairs[i + 1][0] != pairs[i][0]) else 0)
                      + (2 if (ki + 1) * tk - 1 > qi * tq else 0)
                      for i, (qi, ki) in enumerate(pairs)], np.int32)
    grid_spec = pltpu.PrefetchScalarGridSpec(
        num_scalar_prefetch=3,
        grid=(B, H // HS, n),
        in_specs=[pl.BlockSpec((1, HS, tq, E), lambda b, h, p, qt, kt, ft: (b, h, qt[p], 0)),
                  pl.BlockSpec((1, HS, tk, E), lambda b, h, p, qt, kt, ft: (b, h, kt[p], 0)),
                  pl.BlockSpec((1, HS, tk, VA), lambda b, h, p, qt, kt, ft: (b, h, kt[p], 0))],
        out_specs=pl.BlockSpec((1, tq, HS * vdim), lambda b, h, p, qt, kt, ft: (b, qt[p], h)),
        scratch_shapes=[pltpu.VMEM((HS, tq, 1), F32), pltpu.VMEM((HS, tq, VA), F32)],
    )
    return pl.pallas_call(
        functools.partial(_flash_kernel, tq=tq, tk=tk, vdim=vdim, heads=HS),
        grid_spec=grid_spec,
        out_shape=jax.ShapeDtypeStruct((B, T, H * vdim), BF16),
        compiler_params=pltpu.CompilerParams(
            dimension_semantics=("arbitrary", "arbitrary", "arbitrary"), vmem_limit_bytes=VMEM_LIMIT),
    )(jnp.asarray(qi_tab), jnp.asarray(ki_tab), jnp.asarray(flags), q, k, v)


def _paged_kernel(pt_ref, *refs, heads, nope, qk_dim, n_steps, pages, have_rinv):
    del pt_ref
    c_refs, p_refs = refs[:pages], refs[pages:2 * pages]
    qa_ref, qp_ref, x_ref = refs[2 * pages:2 * pages + 3]
    rest = refs[2 * pages + 3:]
    if have_rinv:
        acc_ref, m_ref, l_ref, m_sc, l_sc, acc_sc = rest
    else:
        acc_ref, m_ref, l_ref, rinv_ref, m_sc, l_sc, acc_sc = rest
    step = pl.program_id(1)

    @pl.when(step == 0)
    def _():
        m_sc[...] = jnp.full_like(m_sc, -jnp.inf)
        l_sc[...] = jnp.zeros_like(l_sc)
        acc_sc[...] = jnp.zeros_like(acc_sc)

    c = jnp.concatenate([r[0] for r in c_refs], axis=0).astype(BF16)
    kp = jnp.concatenate([r[0] for r in p_refs], axis=1)
    tokens = c.shape[0]
    if have_rinv:
        rinv = x_ref[0, 0]
    else:
        sub = MXU_DIM
        knts = [_nt(x_ref[...], c[t:t + sub]) for t in range(0, tokens, sub)]
        parts = [jnp.sum((knt * knt).reshape(heads, nope, sub), axis=1) for knt in knts]
        ssq = jnp.concatenate(parts, axis=1)
        kss = jnp.sum(kp * kp, axis=0, keepdims=True)
        rinv = lax.rsqrt((ssq + kss) * (1.0 / qk_dim) + RMS_EPS)
        rinv_ref[0, 0] = rinv
    s = (_nt(qa_ref[0].astype(BF16), c) + _nn(qp_ref[0].astype(BF16), kp.astype(BF16))) * rinv
    m_old = m_sc[...]
    m_new = jnp.maximum(m_old, jnp.max(s, axis=-1, keepdims=True))
    alpha = jnp.exp(m_old - m_new)
    pm = jnp.exp(s - m_new)
    l_sc[...] = alpha * l_sc[...] + jnp.sum(pm, axis=-1, keepdims=True)
    acc_sc[...] = alpha * acc_sc[...] + _nn(pm.astype(BF16), c)
    m_sc[...] = m_new

    @pl.when(step == n_steps - 1)
    def _():
        acc_ref[0] = acc_sc[...]
        m_ref[0] = jnp.broadcast_to(m_sc[...], m_ref.shape[1:])
        l_ref[0] = jnp.broadcast_to(l_sc[...], l_ref.shape[1:])


def _paged_attn(page_table, cache_ckv, cache_kpe_t, qa, qp, nope, *, wnt=None, rinv=None):
    DB, n_pages = page_table.shape
    _, page, R = cache_ckv.shape
    rope = cache_kpe_t.shape[1]
    H = qa.shape[1]
    P = PAGES_PER_STEP
    assert n_pages % P == 0
    n_steps = n_pages // P
    have_rinv = rinv is not None
    page_spec = lambda shape, i: pl.BlockSpec((1,) + shape, lambda b, s, pt: (pt[b, P * s + i], 0, 0))
    rinv_spec = pl.BlockSpec((1, 1, H, P * page), lambda b, s, pt: (b, s, 0, 0))
    x_spec = rinv_spec if have_rinv else pl.BlockSpec(wnt.shape, lambda b, s, pt: (0, 0))
    stat_spec = pl.BlockSpec((1, H, LANES), lambda b, s, pt: (b, 0, 0))
    out_specs = [pl.BlockSpec((1, H, R), lambda b, s, pt: (b, 0, 0)), stat_spec, stat_spec]
    out_shape = [jax.ShapeDtypeStruct((DB, H, R), F32),
                 jax.ShapeDtypeStruct((DB, H, LANES), F32),
                 jax.ShapeDtypeStruct((DB, H, LANES), F32)]
    if not have_rinv:
        out_specs.append(rinv_spec)
        out_shape.append(jax.ShapeDtypeStruct((DB, n_steps, H, P * page), F32))
    grid_spec = pltpu.PrefetchScalarGridSpec(
        num_scalar_prefetch=1,
        grid=(DB, n_steps),
        in_specs=([page_spec((page, R), i) for i in range(P)] + [page_spec((rope, page), i) for i in range(P)]
                  + [pl.BlockSpec((1, H, R), lambda b, s, pt: (b, 0, 0)),
                     pl.BlockSpec((1, H, rope), lambda b, s, pt: (b, 0, 0)), x_spec]),
        out_specs=out_specs,
        scratch_shapes=[pltpu.VMEM((H, 1), F32), pltpu.VMEM((H, 1), F32), pltpu.VMEM((H, R), F32)],
    )
    outs = pl.pallas_call(
        functools.partial(_paged_kernel, heads=H, nope=nope, qk_dim=nope + rope, n_steps=n_steps, pages=P,
                          have_rinv=have_rinv),
        grid_spec=grid_spec,
        out_shape=out_shape,
        compiler_params=pltpu.CompilerParams(
            dimension_semantics=("arbitrary", "arbitrary"), vmem_limit_bytes=VMEM_LIMIT),
    )(page_table, *([cache_ckv] * P), *([cache_kpe_t] * P), qa, qp, rinv if have_rinv else wnt)
    acc, m, l = outs[:3]
    return acc, m[:, :, 0], l[:, :, 0], (rinv if have_rinv else outs[3])


def _pad_cols(w, n):
    return jnp.pad(w, ((0, 0), (0, n - w.shape[1])))


def _pad_rows(w, n):
    return jnp.pad(w, ((0, n - w.shape[0]), (0, 0)))


def _lora(x, w1, w2, act):
    rank = w1.shape[1]
    rp = -(-rank // LANES) * LANES
    mid = _mm(x, _pad_cols(w1, rp), act=act, out_dtype=BF16)
    return _mm(mid, _pad_rows(w2, rp))


def _rope(x, cos, sin):
    half = x.shape[-1] // 2
    x1, x2 = x[..., :half], x[..., half:]
    return jnp.concatenate([x1 * cos - x2 * sin, x2 * cos + x1 * sin], axis=-1)


def kernel(x_prompt, x_sample, state_shift, state_wkv, state_conv, cache_ckv, cache_kpe, page_table, meta_tokens, norm_mix, norm_ffn, mu, w_rkv, w_o_a, w0, w1, w2, a0, a1, a2, v0, v1, v2, g1, g2, k_k, k_a, r_k, lnx_w, lnx_b, ffn_w_in, ffn_conv_w, ffn_conv_b, ffn_w_out, norm_kv, w_dkv, g_ckv, w_ukv, g_k, w_dq, g_q, w_uq, g_qn, w_o_b):
    B, seq, D = x_prompt.shape
    DB = x_sample.shape[0]
    assert x_sample.shape[1] == 1
    n_meta = meta_tokens.shape[0]
    depth = norm_mix.shape[0]
    n_a = mu.shape[0]
    HA, NA = r_k.shape[1], r_k.shape[2]
    assert ffn_conv_w.shape[1] == 3
    R = g_ckv.shape[0]
    rope = w_dkv.shape[1] - R
    HB = w_ukv.shape[1]
    nope = g_k.shape[0] - rope // 2
    vdim = w_ukv.shape[2] - nope
    qk = nope + rope
    T = seq + n_meta
    Tp = -(-T // ROW_ALIGN) * ROW_ALIGN
    Tpa = -(-Tp // ATTN_ALIGN) * ATTN_ALIGN
    MP = B * Tp
    M = MP + DB
    past_len = page_table.shape[1] * cache_ckv.shape[1]
    scale = qk ** -0.5

    h0 = jnp.concatenate([jnp.broadcast_to(meta_tokens[None], (B, n_meta, D)), x_prompt], axis=1)
    h0 = jnp.pad(h0, ((0, 0), (0, Tp - T), (0, 0)))
    h = jnp.concatenate([h0.reshape(MP, D), x_sample.reshape(DB, D)], axis=0)

    t_of_row = jnp.concatenate([jnp.tile(jnp.arange(Tp), B), jnp.full((DB,), past_len)])
    not_first = (t_of_row[:MP] > 0).astype(F32)[:, None]
    inv_freq = ROPE_THETA ** (-jnp.arange(0, rope, 2, dtype=F32) / rope)
    ang = t_of_row.astype(F32)[:, None] * inv_freq[None]
    cos, sin = jnp.cos(ang), jnp.sin(ang)
    last_rows = np.array([b * Tp + T - 1 for b in range(B)])

    state_wkv_t = jnp.transpose(state_wkv, (0, 2, 3, 4, 1))
    cache_kpe_t = jnp.swapaxes(cache_kpe, 1, 2)
    w_rkv3 = w_rkv.reshape(n_a * 3, D, D)
    half = rope // 2
    wn, wr = HB * nope, HB * half
    uq_part = lambda lo, hi: w_uq[..., lo:hi].reshape(w_uq.shape[0], w_uq.shape[1], HB * (hi - lo))
    w_uq3 = jnp.concatenate([uq_part(0, nope), uq_part(nope, nope + half), uq_part(nope + half, qk)], axis=-1)
    cos_t, sin_t = jnp.tile(cos, (1, HB)), jnp.tile(sin, (1, HB))
    q_row_scale = jnp.concatenate([jnp.full((MP, 1), scale * LOG2E, F32), jnp.full((DB, 1), scale, F32)])

    def to_heads(parts):
        t = jnp.concatenate([p[:MP].reshape(B, Tp, HB, -1) for p in parts], axis=-1)
        return jnp.pad(t.transpose(0, 2, 1, 3), ((0, 0), (0, 0), (0, Tpa - Tp), (0, 0)))
    w_ob3 = w_o_b.reshape(w_o_b.shape[0], HB * vdim, D)

    shift_p, shift_s, wkv_p, conv_p, conv_s = [], [], [], [], []
    wkv_s_t = None
    v_first = None
    kv_p = kv_s = None
    ckv = kpe = None

    for i in range(depth):
        if i < n_a:
            xn = _rmsnorm(h, norm_mix[i], F32)
            shift_p.append(xn[last_rows])
            shift_s.append(xn[MP:])
            prev = jnp.concatenate([jnp.roll(xn[:MP], 1, axis=0) * not_first, state_shift[i]], axis=0)
            dx = prev - xn
            xs = [(xn + dx * mu[i, s]).astype(BF16) for s in range(6)]
            r = _mm(xs[0], w_rkv3, layer=3 * i)
            k = _mm(xs[1], w_rkv3, layer=3 * i + 1)
            v = _mm(xs[2], w_rkv3, layer=3 * i + 2)
            wl = _lora(xs[3], w1[i], w2[i], jnp.tanh)
            al = _lora(xs[4], a1[i], a2[i], None)
            g = _lora(xs[5], g1[i], g2[i], jax.nn.sigmoid)
            if i > 0:
                vl = _lora(xs[2], v1[i - 1], v2[i - 1], None)
                log_decay, k, sa, sb, v = _rwkv_pre(k, wl, al, w0[i], a0[i], k_k[i], k_a[i], NA,
                                                    vmix=(v, v_first, vl, v0[i - 1]))
            else:
                v_first = v
                log_decay, k, sa, sb = _rwkv_pre(k, wl, al, w0[i], a0[i], k_k[i], k_a[i], NA)
            y, st_p = _wkv_chunked(r, log_decay, k, v, sa, sb, B, Tp, T, NA)
            lanes_t = lambda t: t[MP:].reshape(DB, HA, NA).transpose(1, 2, 0)
            y_s, wkv_s_t = _wkv_step(*(lanes_t(t) for t in (r, log_decay, k, v, sa, sb)), state_wkv_t, i, wkv_s_t)
            y = lax.dynamic_update_slice(y, y_s.transpose(2, 0, 1).reshape(DB, D), (MP, 0))
            wkv_p.append(st_p)
            gated = _rwkv_out(y, r, k, v, g, lnx_w[i], lnx_b[i], r_k[i], NA)
            h = _mm(gated, w_o_a, layer=i, residual=h)
        else:
            j = i - n_a
            xn = _rmsnorm(h, norm_mix[i], BF16)
            cq = _rmsnorm(_mm(xn, w_dq, layer=j), g_q[j], BF16)
            q2d = _mm(cq, w_uq3, layer=j)
            q_parts = _qk_prep(q2d, 0, q2d, wn // wr, q2d, wn // wr + 1, HB, nope, half, g_qn[j], q_row_scale,
                               cos_t, sin_t)[:3]
            o_p = _flash(to_heads(q_parts), kv_p[0], kv_p[1], vdim,
                         tq=Tpa // FLASH_BLOCKS, tk=Tpa // FLASH_BLOCKS)[:, :Tp]
            qn_s, q1_s, q2_s = (t[MP:].astype(F32).reshape(DB, HB, -1) for t in q_parts)
            qn = qn_s * g_k[:nope]
            qp = jnp.concatenate([q1_s * g_k[nope:], q2_s * g_k[nope:]], axis=-1)
            qa = _bmm(qn.transpose(1, 0, 2), kv_s['wnt3']).transpose(1, 0, 2)
            acc, m, l, kv_s['rinv_cache'] = _paged_attn(
                page_table, cache_ckv, cache_kpe_t, qa, qp, nope, wnt=kv_s['wnt'], rinv=kv_s['rinv_cache'])
            c_new, kp_new = kv_s['ckv'], kv_s['kpe']
            s_new = kv_s['rinv'] * (jnp.sum(qa * c_new[:, None, :], axis=-1) + jnp.sum(qp * kp_new[:, None, :], axis=-1))
            m_f = jnp.maximum(m, s_new)
            alpha = jnp.exp(m - m_f)
            pn = jnp.exp(s_new - m_f)
            l_f = l * alpha + pn
            ctx = (acc * alpha[..., None] + pn[..., None] * c_new[:, None, :]) / l_f[..., None]
            o_s = _bmm(ctx.transpose(1, 0, 2), kv_s['wv']).transpose(1, 0, 2).reshape(DB, HB * vdim)
            attn = jnp.concatenate([o_p.reshape(MP, HB * vdim), o_s.astype(BF16)], axis=0)
            h = _mm(attn, w_ob3, layer=j, residual=h)

        xn = _rmsnorm(h, norm_ffn[i], BF16)
        gated, c_tail, c_s = _ffn_in(xn, ffn_w_in, i, ffn_conv_w[i], ffn_conv_b[i], state_conv[i], B, Tp, T)
        conv_p.append(c_tail)
        conv_s.append(jnp.stack([state_conv[i][:, 1], c_s], axis=1))
        h = _mm(gated, ffn_w_out, layer=i, residual=h)

        if i == n_a - 1:
            xk = _rmsnorm(h, norm_kv, BF16)
            ckv = _rmsnorm(_mm(xk, w_dkv[:, :R]), g_ckv, F32)
            kpe = _rope(_mm(xk, w_dkv[:, R:]), cos, sin)
            w_ukv2 = jnp.concatenate([w_ukv[:, :, :nope].reshape(R, wn), w_ukv[:, :, nope:].reshape(R, HB * vdim)],
                                     axis=1)
            kv2d = _mm(ckv.astype(BF16), w_ukv2)
            kp1 = jnp.tile(kpe[:, :half], (1, HB))
            kp2 = jnp.tile(kpe[:, half:], (1, HB))
            *k_parts, rinv = _qk_prep(kv2d, 0, kp1, 0, kp2, 0, HB, nope, half, g_k, jnp.ones((M, 1), F32))
            rinv = rinv[:, :HB]
            v3 = kv2d[:MP, wn:].reshape(MP, HB, vdim).astype(BF16)
            v_aug = jnp.concatenate([v3, jnp.ones((MP, HB, 1), BF16), jnp.zeros((MP, HB, vdim - 1), BF16)], axis=-1)
            kv_p = (to_heads(k_parts), to_heads([v_aug.reshape(MP, HB * 2 * vdim)]))
            wnt3 = w_ukv[:, :, :nope].transpose(1, 2, 0)
            kv_s = dict(ckv=ckv[MP:], kpe=kpe[MP:], rinv=rinv[MP:], wnt3=wnt3, rinv_cache=None,
                        wnt=wnt3.reshape(HB * nope, R).astype(BF16),
                        wv=w_ukv[:, :, nope:].transpose(1, 0, 2))

    wkv_s = jnp.transpose(wkv_s_t, (0, 4, 1, 2, 3))
    return (h[:MP].reshape(B, Tp, D)[:, n_meta:T], h[MP:].reshape(DB, 1, D),
            jnp.stack(shift_p), jnp.stack(wkv_p), jnp.stack(conv_p),
            ckv[:MP].reshape(B, Tp, R)[:, :T], kpe[:MP].reshape(B, Tp, rope)[:, :T],
            jnp.stack(shift_s), wkv_s, jnp.stack(conv_s),
            ckv[MP:].reshape(DB, 1, R), kpe[MP:].reshape(DB, 1, rope))
```

```python
import functools

import numpy as np
import jax
import jax.numpy as jnp
from jax import lax
from jax.experimental import pallas as pl
from jax.experimental.pallas import tpu as pltpu

F32 = jnp.float32
BF16 = jnp.bfloat16

RMS_EPS = 1e-6
ROPE_THETA = 10000.0
GN_EPS_PER_CHANNEL = 1e-5
LOG2E = 1.4426950408889634
LANES = 128
SUBLANES = 8
MXU_DIM = 256
WKV_CHUNK = 64
WKV_UNITS = 4
WKV_STEP_HEADS = 2
ROW_ALIGN = 64
ATTN_ALIGN = 128
FLASH_BLOCKS = 3
FLASH_HEADS = 2
PAGES_PER_STEP = 16
VMEM_LIMIT = 56 * 1024 * 1024


def _nt(a, b):
    return lax.dot_general(a, b, (((1,), (1,)), ((), ())), preferred_element_type=F32)


def _tn(a, b):
    return lax.dot_general(a, b, (((0,), (0,)), ((), ())), preferred_element_type=F32)


def _nn(a, b):
    return jnp.dot(a, b, preferred_element_type=F32)


def _pick(n, candidates):
    for c in candidates:
        if n % c == 0:
            return c
    return n


def _mm_kernel(x_ref, w_ref, *rest, act, has_res):
    if has_res:
        r_ref, o_ref, wb_ref = rest
    else:
        o_ref, wb_ref = rest

    @pl.when(pl.program_id(1) == 0)
    def _():
        wb_ref[...] = w_ref[...].astype(BF16)

    acc = _nn(x_ref[...].astype(BF16), wb_ref[...])
    if act is not None:
        acc = act(acc)
    if has_res:
        acc = acc + r_ref[...]
    o_ref[...] = acc.astype(o_ref.dtype)


def _mm(x, w, *, layer=None, act=None, residual=None, out_dtype=F32):
    M, K = x.shape
    N = w.shape[-1]
    tm = _pick(M, (768, 512, 384, 256, 128))
    if K > 4096:
        tm = _pick(M, (384, 256, 128))
    tn = _pick(N, (1024, 512, 256, 128)) if K <= 2048 else _pick(N, (512, 256, 128))
    grid = (N // tn, M // tm)
    if layer is None:
        w_spec = pl.BlockSpec((K, tn), lambda j, i: (0, j))
    else:
        w_spec = pl.BlockSpec((None, K, tn), lambda j, i: (layer, 0, j))
    in_specs = [pl.BlockSpec((tm, K), lambda j, i: (i, 0)), w_spec]
    args = [x, w]
    if residual is not None:
        in_specs.append(pl.BlockSpec((tm, tn), lambda j, i: (i, j)))
        args.append(residual)
    return pl.pallas_call(
        functools.partial(_mm_kernel, act=act, has_res=residual is not None),
        grid=grid,
        in_specs=in_specs,
        out_specs=pl.BlockSpec((tm, tn), lambda j, i: (i, j)),
        out_shape=jax.ShapeDtypeStruct((M, N), out_dtype),
        scratch_shapes=[pltpu.VMEM((K, tn), BF16)],
        compiler_params=pltpu.CompilerParams(
            dimension_semantics=("arbitrary", "arbitrary"), vmem_limit_bytes=VMEM_LIMIT),
    )(*args)


def _bmm_kernel(x_ref, w_ref, o_ref):
    o_ref[0] = _nn(x_ref[0].astype(BF16), w_ref[0].astype(BF16))


def _bmm(x, w):
    G, M, K = x.shape
    N = w.shape[2]
    return pl.pallas_call(
        _bmm_kernel,
        grid=(G,),
        in_specs=[pl.BlockSpec((1, M, K), lambda g: (g, 0, 0)),
                  pl.BlockSpec((1, K, N), lambda g: (g, 0, 0))],
        out_specs=pl.BlockSpec((1, M, N), lambda g: (g, 0, 0)),
        out_shape=jax.ShapeDtypeStruct((G, M, N), F32),
    )(x, w)


def _rms_kernel(x_ref, g_ref, o_ref):
    x = x_ref[...].astype(F32)
    y = x * lax.rsqrt(jnp.mean(x * x, axis=-1, keepdims=True) + RMS_EPS)
    o_ref[...] = (y * g_ref[...]).astype(o_ref.dtype)


def _rmsnorm(x, g, out_dtype):
    M, D = x.shape
    tm = _pick(M, (768, 512, 384, 256, 128))
    return pl.pallas_call(
        _rms_kernel,
        grid=(M // tm,),
        in_specs=[pl.BlockSpec((tm, D), lambda i: (i, 0)),
                  pl.BlockSpec((1, D), lambda i: (0, 0))],
        out_specs=pl.BlockSpec((tm, D), lambda i: (i, 0)),
        out_shape=jax.ShapeDtypeStruct((M, D), out_dtype),
    )(x, g.reshape(1, D).astype(F32))


def _norm_mix_kernel(h_ref, g_ref, mu_ref, st_ref, xn_ref, *rest, tm, seq_rows, n_seq, sample_tile):
    outs, xs_sc = rest[:-1], rest[-1]
    i = pl.program_id(0)
    S = SUBLANES

    @pl.when(i == 0)
    def _():
        xs_sc[0:S, :] = jnp.zeros((S, xs_sc.shape[1]), F32)

    x = h_ref[...]
    xn = x * lax.rsqrt(jnp.mean(x * x, axis=-1, keepdims=True) + RMS_EPS) * g_ref[...]
    xn_ref[...] = xn
    xs_sc[S:S + tm, :] = xn
    prev = xs_sc[S - 1:S - 1 + tm, :]
    row = i * tm + lax.broadcasted_iota(jnp.int32, (tm, 1), 0)
    first = row == 0
    for b in range(1, n_seq):
        first = first | (row == b * seq_rows)
    prev = jnp.where(first, 0.0, prev)
    prev = jnp.where(i >= sample_tile, st_ref[...], prev)
    dx = prev - xn
    for s, o_ref in enumerate(outs):
        o_ref[...] = (xn + dx * mu_ref[s:s + 1, :]).astype(o_ref.dtype)
    xs_sc[0:S, :] = xs_sc[tm:tm + S, :]


def _norm_mix(h, g, mu, state, n_seq, seq_rows):
    M, D = h.shape
    MP = n_seq * seq_rows
    tm = M - MP
    n_mix = mu.shape[0]
    assert MP % tm == 0 and tm % SUBLANES == 0 and state.shape == (tm, D)
    row = pl.BlockSpec((tm, D), lambda i: (i, 0))
    full = lambda shape: pl.BlockSpec(shape, lambda i: (0, 0))
    outs = pl.pallas_call(
        functools.partial(_norm_mix_kernel, tm=tm, seq_rows=seq_rows, n_seq=n_seq, sample_tile=MP // tm),
        grid=(M // tm,),
        in_specs=[row, full((1, D)), full((n_mix, D)), full((tm, D))],
        out_specs=[row] * (1 + n_mix),
        out_shape=[jax.ShapeDtypeStruct((M, D), F32)] + [jax.ShapeDtypeStruct((M, D), BF16)] * n_mix,
        scratch_shapes=[pltpu.VMEM((tm + SUBLANES, D), F32)],
        compiler_params=pltpu.CompilerParams(dimension_semantics=("arbitrary",), vmem_limit_bytes=VMEM_LIMIT),
    )(h, g.reshape(1, D), mu, state)
    return outs[0], outs[1:]


def _gate(c, p1, p2, z, cw_ref, cb_ref):
    conv = cb_ref[...] + p2 * cw_ref[0:1, :] + p1 * cw_ref[1:2, :] + c * cw_ref[2:3, :]
    return (jax.nn.silu(conv) * z).astype(BF16)


def _ffn_in_prompt_kernel(x_ref, wc_ref, wz_ref, cw_ref, cb_ref, g_ref, tail_ref, wcb_ref, wzb_ref, cs_ref,
                          *, tm, seq_rows, n_seq, tails):
    i = pl.program_id(1)
    S = SUBLANES

    @pl.when(i == 0)
    def _():
        wcb_ref[...] = wc_ref[...].astype(BF16)
        wzb_ref[...] = wz_ref[...].astype(BF16)
        cs_ref[0:S, :] = jnp.zeros((S, cs_ref.shape[1]), F32)

    x = x_ref[...]
    c = _nn(x, wcb_ref[...])
    z = _nn(x, wzb_ref[...])
    cs_ref[S:S + tm, :] = c
    p1 = cs_ref[S - 1:S - 1 + tm, :]
    p2 = cs_ref[S - 2:S - 2 + tm, :]
    row = i * tm + lax.broadcasted_iota(jnp.int32, (tm, 1), 0)
    t = row
    for b in range(1, n_seq):
        t = jnp.where(row >= b * seq_rows, row - b * seq_rows, t)
    p1 = jnp.where(t >= 1, p1, 0.0)
    p2 = jnp.where(t >= 2, p2, 0.0)
    g_ref[...] = _gate(c, p1, p2, z, cw_ref, cb_ref)
    cs_ref[0:S, :] = cs_ref[tm:tm + S, :]
    for b, (tile, off) in enumerate(tails):
        @pl.when(i == tile)
        def _(b=b, off=off):
            tail_ref[b] = c[off:off + S]


def _ffn_in_sample_kernel(x_ref, wc_ref, wz_ref, cw_ref, cb_ref, b0_ref, b1_ref, alias_ref, g_ref, c_ref):
    del alias_ref
    x = x_ref[...]
    c = _nn(x, wc_ref[...].astype(BF16))
    z = _nn(x, wz_ref[...].astype(BF16))
    c_ref[...] = c
    g_ref[...] = _gate(c, b1_ref[...], b0_ref[...], z, cw_ref, cb_ref)


def _ffn_in(xn, w_in, layer, conv_w, conv_b, buf, n_seq, seq_rows, t_valid):
    M, K = xn.shape
    F = conv_b.shape[0]
    MP = n_seq * seq_rows
    DB = M - MP
    S = SUBLANES
    tn = _pick(F, (512, 256, 128))
    nj = F // tn
    tm = _pick(MP, (640, 512, 384, 256, 128))
    cw = conv_w.astype(F32)
    cb = conv_b.reshape(1, F).astype(F32)
    t0 = t_valid - 2
    assert t0 % S <= S - 2 and MP % DB == 0
    tails = []
    for b in range(n_seq):
        r0 = b * seq_rows + (t0 // S) * S
        assert r0 // tm == (r0 + S - 1) // tm
        tails.append((r0 // tm, r0 % tm))
    wspec = lambda off: pl.BlockSpec((None, K, tn), lambda j, i: (layer, 0, j + off))
    cspec = lambda rows: pl.BlockSpec((rows, tn), lambda j, i: (0, j))
    gated, tail = pl.pallas_call(
        functools.partial(_ffn_in_prompt_kernel, tm=tm, seq_rows=seq_rows, n_seq=n_seq, tails=tuple(tails)),
        grid=(nj, MP // tm),
        in_specs=[pl.BlockSpec((tm, K), lambda j, i: (i, 0)), wspec(0), wspec(nj), cspec(3), cspec(1)],
        out_specs=[pl.BlockSpec((tm, tn), lambda j, i: (i, j)),
                   pl.BlockSpec((n_seq, S, tn), lambda j, i: (0, 0, j))],
        out_shape=[jax.ShapeDtypeStruct((M, F), BF16), jax.ShapeDtypeStruct((n_seq, S, F), F32)],
        scratch_shapes=[pltpu.VMEM((K, tn), BF16), pltpu.VMEM((K, tn), BF16), pltpu.VMEM((tm + S, tn), F32)],
        compiler_params=pltpu.CompilerParams(
            dimension_semantics=("arbitrary", "arbitrary"), vmem_limit_bytes=VMEM_LIMIT),
    )(xn, w_in, w_in, cw, cb)
    wspec1 = lambda off: pl.BlockSpec((None, K, tn), lambda j: (layer, 0, j + off))
    cspec1 = lambda rows: pl.BlockSpec((rows, tn), lambda j: (0, j))
    gated, c_s = pl.pallas_call(
        _ffn_in_sample_kernel,
        grid=(nj,),
        in_specs=[pl.BlockSpec((DB, K), lambda j: (MP // DB, 0)), wspec1(0), wspec1(nj), cspec1(3), cspec1(1),
                  cspec1(DB), cspec1(DB), pl.BlockSpec(memory_space=pl.ANY)],
        out_specs=[pl.BlockSpec((DB, tn), lambda j: (MP // DB, j)), cspec1(DB)],
        out_shape=[jax.ShapeDtypeStruct((M, F), BF16), jax.ShapeDtypeStruct((DB, F), F32)],
        input_output_aliases={7: 0},
        compiler_params=pltpu.CompilerParams(dimension_semantics=("arbitrary",), vmem_limit_bytes=VMEM_LIMIT),
    )(xn, w_in, w_in, cw, cb, buf[:, 0], buf[:, 1], gated)
    off = t0 % S
    return gated, tail[:, off:off + 2], c_s


def _wkv_units(r, lw, k, v, a, b, sts, consts, *, chunk, groups, levels):
    bd, code, tri, eye = consts
    W = MXU_DIM
    units = len(sts)
    hi = lw.astype(BF16)
    rem = lw - hi.astype(F32)
    mid = rem.astype(BF16)
    lo = (rem - mid.astype(F32)).astype(BF16)
    lc = _nn(tri, hi) + _nn(tri, mid) + _nn(tri, lo)
    lc_end = lc[chunk - 1:chunk, :]
    e_neg = jnp.exp(-lc)
    e_end = jnp.exp(lc_end - lc)
    d_end = jnp.exp(lc_end)

    def cut(x):
        return [x[:, u * W:(u + 1) * W] for u in range(units)]

    def each(f, *lists):
        return [f(*xs) for xs in zip(*lists)]

    def stack(x):
        return jnp.concatenate([x] * groups, axis=0) * bd

    def stack_b(x):
        return stack(x).astype(BF16)

    bf = lambda x: x.astype(BF16)
    rt_f = each(stack, cut(r * jnp.exp(lc)))
    rt_s = each(bf, rt_f)
    at_s = each(stack_b, cut(a * jnp.exp(lc - lw)))
    kt_s = each(stack_b, cut(k * e_neg))
    bt_s = each(stack_b, cut(b * e_neg))
    kh_s = each(stack_b, cut(k * e_end))
    bh_s = each(stack_b, cut(b * e_end))
    v_s = each(stack_b, cut(v))
    d_end = cut(d_end)

    strict = (code >= 0) & (code < levels)
    incl = code >= 0
    m_ab = each(_nt, at_s, bt_s)
    m_ak = each(lambda x, y: bf(jnp.where(strict, _nt(x, y), 0.0)), at_s, kt_s)
    n_rb = each(lambda x, y: bf(jnp.where(incl, _nt(x, y), 0.0)), rt_s, bt_s)
    n_rk = each(lambda x, y: bf(jnp.where(incl, _nt(x, y), 0.0)), rt_s, kt_s)
    mv_b = each(lambda x, y: bf(_nn(x, y)), m_ak, v_s)

    inv = each(lambda m: eye + jnp.where(code == 0, m, 0.0), m_ab)
    for lev in range(1, levels):
        inv_b = each(bf, inv)
        off = each(lambda m: bf(jnp.where(code == lev, m, 0.0)), m_ab)
        mid_b = each(lambda o, t: bf(_nn(o, t)), off, inv_b)
        inv = each(lambda t, tb, x: t + _nn(tb, x), inv, inv_b, mid_b)
    inv_b = each(bf, inv)

    p_b = each(lambda t, x: bf(_nn(t, x)), inv_b, at_s)
    q_b = each(lambda t, x: bf(_nn(t, x)), inv_b, mv_b)

    g_b = each(lambda d, x, y: bf(eye * d + _tn(x, y)), d_end, bh_s, p_b)
    f_mat = each(lambda x, y, z, w: _tn(x, y) + _tn(z, w), bh_s, q_b, kh_s, v_s)
    ry_b = each(lambda x, n, p: bf(x + _nn(n, p)), rt_f, n_rb, p_b)
    y_0 = each(lambda n, q, m, w: _nn(n, q) + _nn(m, w), n_rb, q_b, n_rk, v_s)

    st_b = each(bf, sts)
    y_s = each(lambda x, s, y0: _nn(x, s) + y0, ry_b, st_b, y_0)
    new_sts = each(lambda g, s, f: _nn(g, s) + f, g_b, st_b, f_mat)

    def unstack(x):
        y = x[0:chunk]
        for h in range(1, groups):
            y = y + x[h * chunk:(h + 1) * chunk]
        return y

    return jnp.concatenate(each(unstack, y_s), axis=1), new_sts


def _wkv_chunk_kernel(r_ref, lw_ref, k_ref, v_ref, a_ref, b_ref, bd_ref, code_ref, tri_ref, eye_ref,
                      y_ref, s_ref, st_ref, *, chunk, groups, units, t_valid, n_chunks, levels):
    c = pl.program_id(1)

    @pl.when(c == 0)
    def _():
        st_ref[...] = jnp.zeros_like(st_ref)

    row = c * chunk + lax.broadcasted_iota(jnp.int32, (chunk, 1), 0)
    valid = row < t_valid
    consts = (bd_ref[...], code_ref[...], tri_ref[...], eye_ref[...])
    ins = (jnp.where(valid, ref[...], 0.0) for ref in (r_ref, lw_ref, k_ref, v_ref, a_ref, b_ref))
    y, sts = _wkv_units(*ins, [st_ref[u] for u in range(units)], consts, chunk=chunk, groups=groups, levels=levels)
    y_ref[...] = y
    for u in range(units):
        st_ref[u] = sts[u]

    @pl.when(c == n_chunks - 1)
    def _():
        s_ref[0] = st_ref[...]


def _wkv_chunked(r, lw, k, v, a, b, n_seq, seq_rows, t_valid, head):
    M, D = r.shape
    L = WKV_CHUNK
    W = MXU_DIM
    U = WKV_UNITS
    G = W // head
    nu = D // (W * U)
    nc = seq_rows // L
    assert seq_rows % L == 0 and D % (W * U) == 0 and G * L == W
    levels = int(np.log2(L))
    assert 2 ** levels == L
    idx = np.arange(W)
    same = (idx[:, None] // L) == (idx[None, :] // L)
    bd = same.astype(np.float32)
    diff = idx[:, None] ^ idx[None, :]
    code = np.floor(np.log2(np.maximum(diff, 1))).astype(np.int32)
    code = np.where(same & (idx[None, :] < idx[:, None]), code, -1)
    code = np.where(idx[None, :] == idx[:, None], levels, code).astype(np.int32)
    tri = jnp.asarray(np.tril(np.ones((L, L), np.float32)), BF16)
    eye = np.eye(W, dtype=np.float32)

    seq = pl.BlockSpec((L, W * U), lambda p, c: ((p // nu) * nc + c, p % nu))
    const = lambda shape: pl.BlockSpec(shape, lambda p, c: (0, 0))
    y, st = pl.pallas_call(
        functools.partial(_wkv_chunk_kernel, chunk=L, groups=G, units=U, t_valid=t_valid, n_chunks=nc,
                          levels=levels),
        grid=(n_seq * nu, nc),
        in_specs=[seq] * 6 + [const((W, W))] * 2 + [const((L, L)), const((W, W))],
        out_specs=[seq, pl.BlockSpec((1, U, W, W), lambda p, c: (p, 0, 0, 0))],
        out_shape=[jax.ShapeDtypeStruct((M, D), F32),
                   jax.ShapeDtypeStruct((n_seq * nu, U, W, W), F32)],
        scratch_shapes=[pltpu.VMEM((U, W, W), F32)],
        compiler_params=pltpu.CompilerParams(dimension_semantics=("arbitrary", "arbitrary")),
    )(r, lw, k, v, a, b, jnp.asarray(bd), jnp.asarray(code), tri, jnp.asarray(eye))
    ng = nu * U
    st = st.reshape(n_seq, ng, G, head, G, head)
    st = jnp.stack([st[:, :, h, :, h, :] for h in range(G)], axis=2)
    return y, jnp.swapaxes(st, -1, -2).reshape(n_seq, ng * G, head, head)


def _wkv_step_kernel(r_ref, lw_ref, k_ref, v_ref, a_ref, b_ref, s_ref, *rest, heads, head):
    y_ref, so_ref = rest[-2:]
    for h in range(heads):
        w, a, b, k, r, v = (ref[h] for ref in (lw_ref, a_ref, b_ref, k_ref, r_ref, v_ref))
        w = jnp.exp(w)
        ys = []
        for i in range(head):
            s = s_ref[h, i]
            sa = jnp.sum(s * a, axis=0, keepdims=True)
            s_new = s * w + sa * b + v[i:i + 1, :] * k
            so_ref[h, i] = s_new
            ys.append(jnp.sum(s_new * r, axis=0, keepdims=True))
        y_ref[h] = jnp.concatenate(ys, axis=0)


def _wkv_step(r, lw, k, v, a, b, state, layer, prev_out):
    H, N, DB = r.shape
    hb = WKV_STEP_HEADS
    assert H % hb == 0
    vec = pl.BlockSpec((hb, N, DB), lambda g: (g, 0, 0))
    mat = pl.BlockSpec((None, hb, N, N, DB), lambda g: (layer, g, 0, 0, 0))
    in_specs = [vec] * 6 + [mat]
    args = [r, lw, k, v, a, b, state]
    aliases = {}
    if prev_out is not None:
        in_specs.append(pl.BlockSpec(memory_space=pl.ANY))
        args.append(prev_out)
        aliases = {7: 1}
    return pl.pallas_call(
        functools.partial(_wkv_step_kernel, heads=hb, head=N),
        grid=(H // hb,),
        in_specs=in_specs,
        out_specs=[vec, mat],
        out_shape=[jax.ShapeDtypeStruct((H, N, DB), F32), jax.ShapeDtypeStruct(state.shape, F32)],
        input_output_aliases=aliases,
        compiler_params=pltpu.CompilerParams(dimension_semantics=("arbitrary",), vmem_limit_bytes=VMEM_LIMIT),
    )(*args)


def _rwkv_pre_kernel(*refs, head, has_vmix):
    if has_vmix:
        (k_ref, wl_ref, al_ref, v_ref, vf_ref, vl_ref, w0_ref, a0_ref, kk_ref, ka_ref, v0_ref, bd_ref,
         lw_ref, k2_ref, sa_ref, sb_ref, v2_ref) = refs
    else:
        (k_ref, wl_ref, al_ref, w0_ref, a0_ref, kk_ref, ka_ref, bd_ref, lw_ref, k2_ref, sa_ref, sb_ref) = refs
    W = MXU_DIM
    ones_bd = bd_ref[...]
    for c in range(0, k_ref.shape[1], W):
        lanes = slice(c, c + W)
        k = k_ref[:, lanes]
        w_log = -jax.nn.softplus(-(w0_ref[:, lanes] + wl_ref[:, lanes])) - 0.5
        lw_ref[:, lanes] = -jnp.exp(w_log)
        a = jax.nn.sigmoid(a0_ref[:, lanes] + al_ref[:, lanes])
        kk = k * kk_ref[:, lanes]
        kk = kk / jnp.maximum(jnp.sqrt(_head_sum(kk * kk, ones_bd)), 1e-12)
        k2_ref[:, lanes] = k * (1.0 + (a - 1.0) * ka_ref[:, lanes])
        sa_ref[:, lanes] = -kk
        sb_ref[:, lanes] = kk * a
        if has_vmix:
            v = v_ref[:, lanes]
            v2_ref[:, lanes] = v + (vf_ref[:, lanes] - v) * jax.nn.sigmoid(v0_ref[:, lanes] + vl_ref[:, lanes])


def _rwkv_pre(k, wl, al, w0, a0, k_k, k_a, head, vmix=None):
    M, D = k.shape
    W = MXU_DIM
    tm = _pick(M, (256, 128) if vmix is None else (128,))
    idx = np.arange(W)
    ones_bd = jnp.asarray((idx[:, None] // head) == (idx[None, :] // head), BF16)
    row = pl.BlockSpec((tm, D), lambda i: (i, 0))
    vec = pl.BlockSpec((1, D), lambda i: (0, 0))
    bd = pl.BlockSpec((W, W), lambda i: (0, 0))
    as_row = lambda p: p.reshape(1, D)
    if vmix is None:
        args = [k, wl, al, as_row(w0), as_row(a0), as_row(k_k), as_row(k_a), ones_bd]
        in_specs = [row] * 3 + [vec] * 4 + [bd]
        n_out = 4
    else:
        v, v_first, vl, v0 = vmix
        args = [k, wl, al, v, v_first, vl, as_row(w0), as_row(a0), as_row(k_k), as_row(k_a), as_row(v0), ones_bd]
        in_specs = [row] * 6 + [vec] * 5 + [bd]
        n_out = 5
    return pl.pallas_call(
        functools.partial(_rwkv_pre_kernel, head=head, has_vmix=vmix is not None),
        grid=(M // tm,),
        in_specs=in_specs,
        out_specs=[row] * n_out,
        out_shape=[jax.ShapeDtypeStruct((M, D), F32)] * n_out,
        compiler_params=pltpu.CompilerParams(dimension_semantics=("arbitrary",), vmem_limit_bytes=VMEM_LIMIT),
    )(*args)


def _head_sum(x, ones_bd):
    hi = x.astype(BF16)
    lo = (x - hi.astype(F32)).astype(BF16)
    return _nn(hi, ones_bd) + _nn(lo, ones_bd)


def _rwkv_out_kernel(y_ref, r_ref, k_ref, v_ref, g_ref, lw_ref, lb_ref, rk_ref, bd_ref, o_ref, *, head, eps):
    W = MXU_DIM
    ones_bd = bd_ref[...]
    inv_n = 1.0 / head
    for c in range(0, y_ref.shape[1], W):
        lanes = slice(c, c + W)
        y = y_ref[:, lanes]
        d = y - _head_sum(y, ones_bd) * inv_n
        var = _head_sum(d * d, ones_bd) * inv_n
        yn = d * lax.rsqrt(var + eps) * lw_ref[:, lanes] + lb_ref[:, lanes]
        bonus = _head_sum(r_ref[:, lanes] * k_ref[:, lanes] * rk_ref[:, lanes], ones_bd) * v_ref[:, lanes]
        o_ref[:, lanes] = ((yn + bonus) * g_ref[:, lanes]).astype(o_ref.dtype)


def _rwkv_out(y, r, k, v, g, lnx_w, lnx_b, r_k, head):
    M, D = y.shape
    W = MXU_DIM
    tm = _pick(M, (256, 128))
    idx = np.arange(W)
    ones_bd = jnp.asarray((idx[:, None] // head) == (idx[None, :] // head), BF16)
    row = pl.BlockSpec((tm, D), lambda i: (i, 0))
    vec = pl.BlockSpec((1, D), lambda i: (0, 0))
    return pl.pallas_call(
        functools.partial(_rwkv_out_kernel, head=head, eps=head * GN_EPS_PER_CHANNEL),
        grid=(M // tm,),
        in_specs=[row] * 5 + [vec] * 3 + [pl.BlockSpec((W, W), lambda i: (0, 0))],
        out_specs=row,
        out_shape=jax.ShapeDtypeStruct((M, D), BF16),
        compiler_params=pltpu.CompilerParams(dimension_semantics=("arbitrary",), vmem_limit_bytes=VMEM_LIMIT),
    )(y, r, k, v, g, lnx_w.reshape(1, D), lnx_b.reshape(1, D), r_k.reshape(1, D), ones_bd)


def _split_dot(x, w):
    hi = x.astype(BF16)
    lo = (x - hi.astype(F32)).astype(BF16)
    return _nn(hi, w) + _nn(lo, w)


def _qk_prep_kernel(*refs, qk_dim, rope):
    if rope:
        (n_ref, x1_ref, x2_ref, c_ref, s_ref, rs_ref, gn_ref, g1_ref, g2_ref, sn_ref, sr_ref, snt_ref, srt_ref,
         on_ref, o1_ref, o2_ref, ri_ref) = refs
        x1, x2, c, s = x1_ref[...], x2_ref[...], c_ref[...], s_ref[...]
        r1 = x1 * c - x2 * s
        r2 = x2 * c + x1 * s
    else:
        (n_ref, x1_ref, x2_ref, rs_ref, gn_ref, g1_ref, g2_ref, sn_ref, sr_ref, snt_ref, srt_ref,
         on_ref, o1_ref, o2_ref, ri_ref) = refs
        r1, r2 = x1_ref[...], x2_ref[...]
    n = n_ref[...]
    ssq = _split_dot(n * n, sn_ref[...]) + _split_dot(r1 * r1 + r2 * r2, sr_ref[...])
    rinv = lax.rsqrt(ssq * (1.0 / qk_dim) + RMS_EPS)
    ri_ref[...] = rinv
    rinv = rinv * rs_ref[...]
    rn = _split_dot(rinv, snt_ref[...])
    rr = _split_dot(rinv, srt_ref[...])
    on_ref[...] = (n * rn * gn_ref[...]).astype(on_ref.dtype)
    o1_ref[...] = (r1 * rr * g1_ref[...]).astype(o1_ref.dtype)
    o2_ref[...] = (r2 * rr * g2_ref[...]).astype(o2_ref.dtype)


def _qk_prep(src_n, col_n, src_1, col_1, src_2, col_2, heads, nope, half, gain, row_scale, cos_t=None, sin_t=None):
    M = src_n.shape[0]
    wn, wr = heads * nope, heads * half
    tm = _pick(M, (256, 128))
    seg = lambda width, per: jnp.asarray(
        (np.arange(width)[:, None] // per) == np.arange(LANES)[None, :], BF16)
    sn, sr = seg(wn, nope), seg(wr, half)
    gn = jnp.tile(gain[:nope], heads).reshape(1, wn)
    gr = jnp.tile(gain[nope:], heads).reshape(1, wr)
    rows = lambda width, col: pl.BlockSpec((tm, width), lambda i: (i, col))
    full = lambda shape: pl.BlockSpec(shape, lambda i: (0, 0))
    rope = cos_t is not None
    args = [src_n, src_1, src_2] + ([cos_t, sin_t] if rope else []) + [row_scale, gn, gr, gr, sn, sr, sn.T, sr.T]
    in_specs = ([rows(wn, col_n), rows(wr, col_1), rows(wr, col_2)] + ([rows(wr, 0)] * 2 if rope else [])
                + [rows(1, 0), full((1, wn)), full((1, wr)), full((1, wr)),
                   full((wn, LANES)), full((wr, LANES)), full((LANES, wn)), full((LANES, wr))])
    return pl.pallas_call(
        functools.partial(_qk_prep_kernel, qk_dim=nope + 2 * half, rope=rope),
        grid=(M // tm,),
        in_specs=in_specs,
        out_specs=[rows(wn, 0), rows(wr, 0), rows(wr, 0), rows(LANES, 0)],
        out_shape=[jax.ShapeDtypeStruct((M, wn), BF16), jax.ShapeDtypeStruct((M, wr), BF16),
                   jax.ShapeDtypeStruct((M, wr), BF16), jax.ShapeDtypeStruct((M, LANES), F32)],
        compiler_params=pltpu.CompilerParams(dimension_semantics=("arbitrary",), vmem_limit_bytes=VMEM_LIMIT),
    )(*args)


def _flash_kernel(qi_ref, ki_ref, flag_ref, q_ref, k_ref, v_ref, o_ref, m_sc, acc_sc, *, tq, tk, vdim, heads):
    p = pl.program_id(2)
    qi = qi_ref[p]
    ki = ki_ref[p]
    flags = flag_ref[p]
    hs = range(heads)

    @pl.when(ki == 0)
    def _():
        m_sc[...] = jnp.full_like(m_sc, -jnp.inf)
        acc_sc[...] = jnp.zeros_like(acc_sc)

    def update(masked):
        s = [_nt(q_ref[0, h], k_ref[0, h]) for h in hs]
        if masked:
            qpos = qi * tq + lax.broadcasted_iota(jnp.int32, (tq, 1), 0)
            kpos = ki * tk + lax.broadcasted_iota(jnp.int32, (1, tk), 1)
            keep = kpos <= qpos
            s = [jnp.where(keep, x, -jnp.inf) for x in s]
        m_old = [m_sc[h] for h in hs]
        m_new = [jnp.maximum(mo, jnp.max(x, axis=-1, keepdims=True)) for mo, x in zip(m_old, s)]
        pm = [jnp.exp2(x - mn).astype(BF16) for x, mn in zip(s, m_new)]
        for h in hs:
            acc_sc[h] = jnp.exp2(m_old[h] - m_new[h]) * acc_sc[h] + _nn(pm[h], v_ref[0, h])
            m_sc[h] = m_new[h]

    pl.when((flags & 2) != 0)(lambda: update(True))
    pl.when((flags & 2) == 0)(lambda: update(False))

    @pl.when((flags & 1) != 0)
    def _():
        outs = []
        for h in hs:
            acc = acc_sc[h]
            outs.append(acc[:, :vdim] / acc[:, vdim:vdim + 1])
        o_ref[0] = jnp.concatenate(outs, axis=1).astype(o_ref.dtype)


def _flash(q, k, v, vdim, tq, tk):
    B, H, T, E = q.shape
    VA = v.shape[-1]
    HS = FLASH_HEADS
    assert H % HS == 0
    pairs = [(qi, ki) for qi in range(T // tq) for ki in range(T // tk) if ki * tk <= qi * tq + tq - 1]
    n = len(pairs)
    qi_tab = np.array([p[0] for p in pairs], np.int32)
    ki_tab = np.array([p[1] for p in pairs], np.int32)
    flags = np.array([(1 if (i + 1 == n or pairs[i + 1][0] != pairs[i][0]) else 0)
                      + (2 if (ki + 1) * tk - 1 > qi * tq else 0)
                      for i, (qi, ki) in enumerate(pairs)], np.int32)
    grid_spec = pltpu.PrefetchScalarGridSpec(
        num_scalar_prefetch=3,
        grid=(B, H // HS, n),
        in_specs=[pl.BlockSpec((1, HS, tq, E), lambda b, h, p, qt, kt, ft: (b, h, qt[p], 0)),
                  pl.BlockSpec((1, HS, tk, E), lambda b, h, p, qt, kt, ft: (b, h, kt[p], 0)),
                  pl.BlockSpec((1, HS, tk, VA), lambda b, h, p, qt, kt, ft: (b, h, kt[p], 0))],
        out_specs=pl.BlockSpec((1, tq, HS * vdim), lambda b, h, p, qt, kt, ft: (b, qt[p], h)),
        scratch_shapes=[pltpu.VMEM((HS, tq, 1), F32), pltpu.VMEM((HS, tq, VA), F32)],
    )
    return pl.pallas_call(
        functools.partial(_flash_kernel, tq=tq, tk=tk, vdim=vdim, heads=HS),
        grid_spec=grid_spec,
        out_shape=jax.ShapeDtypeStruct((B, T, H * vdim), BF16),
        compiler_params=pltpu.CompilerParams(
            dimension_semantics=("arbitrary", "arbitrary", "arbitrary"), vmem_limit_bytes=VMEM_LIMIT),
    )(jnp.asarray(qi_tab), jnp.asarray(ki_tab), jnp.asarray(flags), q, k, v)


def _paged_kernel(pt_ref, *refs, heads, nope, qk_dim, n_steps, pages, have_rinv):
    del pt_ref
    c_refs, p_refs = refs[:pages], refs[pages:2 * pages]
    qa_ref, qp_ref, x_ref = refs[2 * pages:2 * pages + 3]
    rest = refs[2 * pages + 3:]
    if have_rinv:
        acc_ref, m_ref, l_ref, m_sc, l_sc, acc_sc = rest
    else:
        acc_ref, m_ref, l_ref, rinv_ref, m_sc, l_sc, acc_sc, lhs_sc = rest
    step = pl.program_id(1)
    rows_w = heads * nope

    @pl.when(step == 0)
    def _():
        m_sc[...] = jnp.full_like(m_sc, -jnp.inf)
        l_sc[...] = jnp.zeros_like(l_sc)
        acc_sc[...] = jnp.zeros_like(acc_sc)
        if not have_rinv:
            lhs_sc[rows_w:rows_w + heads, :] = qa_ref[0].astype(BF16)

    if not have_rinv:
        @pl.when((step == 0) & (pl.program_id(0) == 0))
        def _():
            lhs_sc[0:rows_w, :] = x_ref[...]

    c = jnp.concatenate([r[0] for r in c_refs], axis=0).astype(BF16)
    kp = jnp.concatenate([r[0] for r in p_refs], axis=1)
    tokens = c.shape[0]
    if have_rinv:
        rinv = x_ref[0, 0]
        s_nope = _nt(qa_ref[0].astype(BF16), c)
    else:
        sub = MXU_DIM
        knts = [_nt(lhs_sc[...], c[t:t + sub]) for t in range(0, tokens, sub)]
        parts = [jnp.sum((knt[:rows_w] * knt[:rows_w]).reshape(heads, nope, sub), axis=1) for knt in knts]
        ssq = jnp.concatenate(parts, axis=1)
        s_nope = jnp.concatenate([knt[rows_w:] for knt in knts], axis=1)
        kss = jnp.sum(kp * kp, axis=0, keepdims=True)
        rinv = lax.rsqrt((ssq + kss) * (1.0 / qk_dim) + RMS_EPS)
        rinv_ref[0, 0] = rinv
    s = (s_nope + _nn(qp_ref[0].astype(BF16), kp.astype(BF16))) * rinv
    m_old = m_sc[...]
    m_new = jnp.maximum(m_old, jnp.max(s, axis=-1, keepdims=True))
    alpha = jnp.exp(m_old - m_new)
    pm = jnp.exp(s - m_new)
    l_sc[...] = alpha * l_sc[...] + jnp.sum(pm, axis=-1, keepdims=True)
    acc_sc[...] = alpha * acc_sc[...] + _nn(pm.astype(BF16), c)
    m_sc[...] = m_new

    @pl.when(step == n_steps - 1)
    def _():
        acc_ref[0] = acc_sc[...]
        m_ref[0] = jnp.broadcast_to(m_sc[...], m_ref.shape[1:])
        l_ref[0] = jnp.broadcast_to(l_sc[...], l_ref.shape[1:])


def _paged_attn(page_table, cache_ckv, cache_kpe_t, qa, qp, nope, *, wnt=None, rinv=None):
    DB, n_pages = page_table.shape
    _, page, R = cache_ckv.shape
    rope = cache_kpe_t.shape[1]
    H = qa.shape[1]
    P = PAGES_PER_STEP
    assert n_pages % P == 0
    n_steps = n_pages // P
    have_rinv = rinv is not None
    page_spec = lambda shape, i: pl.BlockSpec((1,) + shape, lambda b, s, pt: (pt[b, P * s + i], 0, 0))
    rinv_spec = pl.BlockSpec((1, 1, H, P * page), lambda b, s, pt: (b, s, 0, 0))
    x_spec = rinv_spec if have_rinv else pl.BlockSpec(wnt.shape, lambda b, s, pt: (0, 0))
    stat_spec = pl.BlockSpec((1, H, LANES), lambda b, s, pt: (b, 0, 0))
    out_specs = [pl.BlockSpec((1, H, R), lambda b, s, pt: (b, 0, 0)), stat_spec, stat_spec]
    out_shape = [jax.ShapeDtypeStruct((DB, H, R), F32),
                 jax.ShapeDtypeStruct((DB, H, LANES), F32),
                 jax.ShapeDtypeStruct((DB, H, LANES), F32)]
    scratch = [pltpu.VMEM((H, 1), F32), pltpu.VMEM((H, 1), F32), pltpu.VMEM((H, R), F32)]
    if not have_rinv:
        out_specs.append(rinv_spec)
        out_shape.append(jax.ShapeDtypeStruct((DB, n_steps, H, P * page), F32))
        scratch.append(pltpu.VMEM((H * nope + H, R), BF16))
    grid_spec = pltpu.PrefetchScalarGridSpec(
        num_scalar_prefetch=1,
        grid=(DB, n_steps),
        in_specs=([page_spec((page, R), i) for i in range(P)] + [page_spec((rope, page), i) for i in range(P)]
                  + [pl.BlockSpec((1, H, R), lambda b, s, pt: (b, 0, 0)),
                     pl.BlockSpec((1, H, rope), lambda b, s, pt: (b, 0, 0)), x_spec]),
        out_specs=out_specs,
        scratch_shapes=scratch,
    )
    outs = pl.pallas_call(
        functools.partial(_paged_kernel, heads=H, nope=nope, qk_dim=nope + rope, n_steps=n_steps, pages=P,
                          have_rinv=have_rinv),
        grid_spec=grid_spec,
        out_shape=out_shape,
        compiler_params=pltpu.CompilerParams(
            dimension_semantics=("arbitrary", "arbitrary"), vmem_limit_bytes=VMEM_LIMIT),
    )(page_table, *([cache_ckv] * P), *([cache_kpe_t] * P), qa, qp, rinv if have_rinv else wnt)
    acc, m, l = outs[:3]
    return acc, m[:, :, 0], l[:, :, 0], (rinv if have_rinv else outs[3])


def _pad_cols(w, n):
    return jnp.pad(w, ((0, 0), (0, n - w.shape[1])))


def _pad_rows(w, n):
    return jnp.pad(w, ((0, n - w.shape[0]), (0, 0)))


def _lora(x, w1, w2, act):
    rank = w1.shape[1]
    rp = -(-rank // LANES) * LANES
    mid = _mm(x, _pad_cols(w1, rp), act=act, out_dtype=BF16)
    return _mm(mid, _pad_rows(w2, rp))


def _rope(x, cos, sin):
    half = x.shape[-1] // 2
    x1, x2 = x[..., :half], x[..., half:]
    return jnp.concatenate([x1 * cos - x2 * sin, x2 * cos + x1 * sin], axis=-1)


def kernel(x_prompt, x_sample, state_shift, state_wkv, state_conv, cache_ckv, cache_kpe, page_table, meta_tokens, norm_mix, norm_ffn, mu, w_rkv, w_o_a, w0, w1, w2, a0, a1, a2, v0, v1, v2, g1, g2, k_k, k_a, r_k, lnx_w, lnx_b, ffn_w_in, ffn_conv_w, ffn_conv_b, ffn_w_out, norm_kv, w_dkv, g_ckv, w_ukv, g_k, w_dq, g_q, w_uq, g_qn, w_o_b):
    B, seq, D = x_prompt.shape
    DB = x_sample.shape[0]
    assert x_sample.shape[1] == 1
    n_meta = meta_tokens.shape[0]
    depth = norm_mix.shape[0]
    n_a = mu.shape[0]
    HA, NA = r_k.shape[1], r_k.shape[2]
    assert ffn_conv_w.shape[1] == 3
    R = g_ckv.shape[0]
    rope = w_dkv.shape[1] - R
    HB = w_ukv.shape[1]
    nope = g_k.shape[0] - rope // 2
    vdim = w_ukv.shape[2] - nope
    qk = nope + rope
    T = seq + n_meta
    Tp = -(-T // ROW_ALIGN) * ROW_ALIGN
    Tpa = -(-Tp // ATTN_ALIGN) * ATTN_ALIGN
    MP = B * Tp
    M = MP + DB
    past_len = page_table.shape[1] * cache_ckv.shape[1]
    scale = qk ** -0.5

    h0 = jnp.concatenate([jnp.broadcast_to(meta_tokens[None], (B, n_meta, D)), x_prompt], axis=1)
    h0 = jnp.pad(h0, ((0, 0), (0, Tp - T), (0, 0)))
    h = jnp.concatenate([h0.reshape(MP, D), x_sample.reshape(DB, D)], axis=0)

    t_of_row = jnp.concatenate([jnp.tile(jnp.arange(Tp), B), jnp.full((DB,), past_len)])
    inv_freq = ROPE_THETA ** (-jnp.arange(0, rope, 2, dtype=F32) / rope)
    ang = t_of_row.astype(F32)[:, None] * inv_freq[None]
    cos, sin = jnp.cos(ang), jnp.sin(ang)
    last_rows = np.array([b * Tp + T - 1 for b in range(B)])

    state_wkv_t = jnp.transpose(state_wkv, (0, 2, 3, 4, 1))
    cache_kpe_t = jnp.swapaxes(cache_kpe, 1, 2)
    w_rkv3 = w_rkv.reshape(n_a * 3, D, D)
    half = rope // 2
    wn, wr = HB * nope, HB * half
    uq_part = lambda lo, hi: w_uq[..., lo:hi].reshape(w_uq.shape[0], w_uq.shape[1], HB * (hi - lo))
    w_uq3 = jnp.concatenate([uq_part(0, nope), uq_part(nope, nope + half), uq_part(nope + half, qk)], axis=-1)
    cos_t, sin_t = jnp.tile(cos, (1, HB)), jnp.tile(sin, (1, HB))
    q_row_scale = jnp.concatenate([jnp.full((MP, 1), scale * LOG2E, F32), jnp.full((DB, 1), scale, F32)])

    def to_heads(parts):
        t = jnp.concatenate([p[:MP].reshape(B, Tp, HB, -1) for p in parts], axis=-1)
        return jnp.pad(t.transpose(0, 2, 1, 3), ((0, 0), (0, 0), (0, Tpa - Tp), (0, 0)))
    w_ob3 = w_o_b.reshape(w_o_b.shape[0], HB * vdim, D)

    shift_p, shift_s, wkv_p, conv_p, conv_s = [], [], [], [], []
    wkv_s_t = None
    v_first = None
    kv_p = kv_s = None
    ckv = kpe = None

    for i in range(depth):
        if i < n_a:
            xn, xs = _norm_mix(h, norm_mix[i], mu[i], state_shift[i], B, Tp)
            shift_p.append(xn[last_rows])
            shift_s.append(xn[MP:])
            r = _mm(xs[0], w_rkv3, layer=3 * i)
            k = _mm(xs[1], w_rkv3, layer=3 * i + 1)
            v = _mm(xs[2], w_rkv3, layer=3 * i + 2)
            wl = _lora(xs[3], w1[i], w2[i], jnp.tanh)
            al = _lora(xs[4], a1[i], a2[i], None)
            g = _lora(xs[5], g1[i], g2[i], jax.nn.sigmoid)
            if i > 0:
                vl = _lora(xs[2], v1[i - 1], v2[i - 1], None)
                log_decay, k, sa, sb, v = _rwkv_pre(k, wl, al, w0[i], a0[i], k_k[i], k_a[i], NA,
                                                    vmix=(v, v_first, vl, v0[i - 1]))
            else:
                v_first = v
                log_decay, k, sa, sb = _rwkv_pre(k, wl, al, w0[i], a0[i], k_k[i], k_a[i], NA)
            y, st_p = _wkv_chunked(r, log_decay, k, v, sa, sb, B, Tp, T, NA)
            lanes_t = lambda t: t[MP:].reshape(DB, HA, NA).transpose(1, 2, 0)
            y_s, wkv_s_t = _wkv_step(*(lanes_t(t) for t in (r, log_decay, k, v, sa, sb)), state_wkv_t, i, wkv_s_t)
            y = lax.dynamic_update_slice(y, y_s.transpose(2, 0, 1).reshape(DB, D), (MP, 0))
            wkv_p.append(st_p)
            gated = _rwkv_out(y, r, k, v, g, lnx_w[i], lnx_b[i], r_k[i], NA)
            h = _mm(gated, w_o_a, layer=i, residual=h)
        else:
            j = i - n_a
            xn = _rmsnorm(h, norm_mix[i], BF16)
            cq = _rmsnorm(_mm(xn, w_dq, layer=j), g_q[j], BF16)
            q2d = _mm(cq, w_uq3, layer=j)
            q_parts = _qk_prep(q2d, 0, q2d, wn // wr, q2d, wn // wr + 1, HB, nope, half, g_qn[j], q_row_scale,
                               cos_t, sin_t)[:3]
            o_p = _flash(to_heads(q_parts), kv_p[0], kv_p[1], vdim,
                         tq=Tpa // FLASH_BLOCKS, tk=Tpa // FLASH_BLOCKS)[:, :Tp]
            qn_s, q1_s, q2_s = (t[MP:].astype(F32).reshape(DB, HB, -1) for t in q_parts)
            qn = qn_s * g_k[:nope]
            qp = jnp.concatenate([q1_s * g_k[nope:], q2_s * g_k[nope:]], axis=-1)
            qa = _bmm(qn.transpose(1, 0, 2), kv_s['wnt3']).transpose(1, 0, 2)
            acc, m, l, kv_s['rinv_cache'] = _paged_attn(
                page_table, cache_ckv, cache_kpe_t, qa, qp, nope, wnt=kv_s['wnt'], rinv=kv_s['rinv_cache'])
            c_new, kp_new = kv_s['ckv'], kv_s['kpe']
            s_new = kv_s['rinv'] * (jnp.sum(qa * c_new[:, None, :], axis=-1) + jnp.sum(qp * kp_new[:, None, :], axis=-1))
            m_f = jnp.maximum(m, s_new)
            alpha = jnp.exp(m - m_f)
            pn = jnp.exp(s_new - m_f)
            l_f = l * alpha + pn
            ctx = (acc * alpha[..., None] + pn[..., None] * c_new[:, None, :]) / l_f[..., None]
            o_s = _bmm(ctx.transpose(1, 0, 2), kv_s['wv']).transpose(1, 0, 2).reshape(DB, HB * vdim)
            attn = jnp.concatenate([o_p.reshape(MP, HB * vdim), o_s.astype(BF16)], axis=0)
            h = _mm(attn, w_ob3, layer=j, residual=h)

        xn = _rmsnorm(h, norm_ffn[i], BF16)
        gated, c_tail, c_s = _ffn_in(xn, ffn_w_in, i, ffn_conv_w[i], ffn_conv_b[i], state_conv[i], B, Tp, T)
        conv_p.append(c_tail)
        conv_s.append(jnp.stack([state_conv[i][:, 1], c_s], axis=1))
        h = _mm(gated, ffn_w_out, layer=i, residual=h)

        if i == n_a - 1:
            xk = _rmsnorm(h, norm_kv, BF16)
            ckv = _rmsnorm(_mm(xk, w_dkv[:, :R]), g_ckv, F32)
            kpe = _rope(_mm(xk, w_dkv[:, R:]), cos, sin)
            w_ukv2 = jnp.concatenate([w_ukv[:, :, :nope].reshape(R, wn), w_ukv[:, :, nope:].reshape(R, HB * vdim)],
                                     axis=1)
            kv2d = _mm(ckv.astype(BF16), w_ukv2)
            kp1 = jnp.tile(kpe[:, :half], (1, HB))
            kp2 = jnp.tile(kpe[:, half:], (1, HB))
            *k_parts, rinv = _qk_prep(kv2d, 0, kp1, 0, kp2, 0, HB, nope, half, g_k, jnp.ones((M, 1), F32))
            rinv = rinv[:, :HB]
            v3 = kv2d[:MP, wn:].reshape(MP, HB, vdim).astype(BF16)
            v_aug = jnp.concatenate([v3, jnp.ones((MP, HB, 1), BF16), jnp.zeros((MP, HB, vdim - 1), BF16)], axis=-1)
            kv_p = (to_heads(k_parts), to_heads([v_aug.reshape(MP, HB * 2 * vdim)]))
            wnt3 = w_ukv[:, :, :nope].transpose(1, 2, 0)
            kv_s = dict(ckv=ckv[MP:], kpe=kpe[MP:], rinv=rinv[MP:], wnt3=wnt3, rinv_cache=None,
                        wnt=wnt3.reshape(HB * nope, R).astype(BF16),
                        wv=w_ukv[:, :, nope:].transpose(1, 0, 2))

    wkv_s = jnp.transpose(wkv_s_t, (0, 4, 1, 2, 3))
    return (h[:MP].reshape(B, Tp, D)[:, n_meta:T], h[MP:].reshape(DB, 1, D),
            jnp.stack(shift_p), jnp.stack(wkv_p), jnp.stack(conv_p),
            ckv[:MP].reshape(B, Tp, R)[:, :T], kpe[:MP].reshape(B, Tp, rope)[:, :T],
            jnp.stack(shift_s), wkv_s, jnp.stack(conv_s),
            ckv[MP:].reshape(DB, 1, R), kpe[MP:].reshape(DB, 1, rope))
```

```python
import functools

import numpy as np
import jax
import jax.numpy as jnp
from jax import lax
from jax.experimental import pallas as pl
from jax.experimental.pallas import tpu as pltpu

F32 = jnp.float32
BF16 = jnp.bfloat16

RMS_EPS = 1e-6
ROPE_THETA = 10000.0
GN_EPS_PER_CHANNEL = 1e-5
LOG2E = 1.4426950408889634
LANES = 128
SUBLANES = 8
MXU_DIM = 256
WKV_CHUNK = 64
WKV_UNITS = 8
WKV_STEP_HEADS = 2
ROW_ALIGN = 64
ATTN_ALIGN = 128
FLASH_BLOCKS = 3
FLASH_HEADS = 2
PAGES_PER_STEP = 16
PAGE_STEP_RATIO = 2
VMEM_LIMIT = 56 * 1024 * 1024


def _nt(a, b):
    return lax.dot_general(a, b, (((1,), (1,)), ((), ())), preferred_element_type=F32)


def _tn(a, b):
    return lax.dot_general(a, b, (((0,), (0,)), ((), ())), preferred_element_type=F32)


def _nn(a, b):
    return jnp.dot(a, b, preferred_element_type=F32)


def _pick(n, candidates):
    for c in candidates:
        if n % c == 0:
            return c
    return n


def _mm_kernel(x_ref, w_ref, *rest, act, has_res):
    if has_res:
        r_ref, o_ref, wb_ref = rest
    else:
        o_ref, wb_ref = rest

    @pl.when(pl.program_id(1) == 0)
    def _():
        wb_ref[...] = w_ref[...].astype(BF16)

    acc = _nn(x_ref[...].astype(BF16), wb_ref[...])
    if act is not None:
        acc = act(acc)
    if has_res:
        acc = acc + r_ref[...]
    o_ref[...] = acc.astype(o_ref.dtype)


def _mm(x, w, *, layer=None, act=None, residual=None, out_dtype=F32):
    M, K = x.shape
    N = w.shape[-1]
    tm = _pick(M, (768, 512, 384, 256, 128))
    if K > 4096:
        tm = _pick(M, (384, 256, 128))
    tn = _pick(N, (1024, 512, 256, 128)) if K <= 2048 else _pick(N, (512, 256, 128))
    grid = (N // tn, M // tm)
    if layer is None:
        w_spec = pl.BlockSpec((K, tn), lambda j, i: (0, j))
    else:
        w_spec = pl.BlockSpec((None, K, tn), lambda j, i: (layer, 0, j))
    in_specs = [pl.BlockSpec((tm, K), lambda j, i: (i, 0)), w_spec]
    args = [x, w]
    if residual is not None:
        in_specs.append(pl.BlockSpec((tm, tn), lambda j, i: (i, j)))
        args.append(residual)
    return pl.pallas_call(
        functools.partial(_mm_kernel, act=act, has_res=residual is not None),
        grid=grid,
        in_specs=in_specs,
        out_specs=pl.BlockSpec((tm, tn), lambda j, i: (i, j)),
        out_shape=jax.ShapeDtypeStruct((M, N), out_dtype),
        scratch_shapes=[pltpu.VMEM((K, tn), BF16)],
        compiler_params=pltpu.CompilerParams(
            dimension_semantics=("arbitrary", "arbitrary"), vmem_limit_bytes=VMEM_LIMIT),
    )(*args)


def _bmm_kernel(x_ref, w_ref, o_ref):
    o_ref[0] = _nn(x_ref[0].astype(BF16), w_ref[0].astype(BF16))


def _bmm(x, w):
    G, M, K = x.shape
    N = w.shape[2]
    return pl.pallas_call(
        _bmm_kernel,
        grid=(G,),
        in_specs=[pl.BlockSpec((1, M, K), lambda g: (g, 0, 0)),
                  pl.BlockSpec((1, K, N), lambda g: (g, 0, 0))],
        out_specs=pl.BlockSpec((1, M, N), lambda g: (g, 0, 0)),
        out_shape=jax.ShapeDtypeStruct((G, M, N), F32),
    )(x, w)


def _rms_kernel(x_ref, g_ref, o_ref):
    x = x_ref[...].astype(F32)
    y = x * lax.rsqrt(jnp.mean(x * x, axis=-1, keepdims=True) + RMS_EPS)
    o_ref[...] = (y * g_ref[...]).astype(o_ref.dtype)


def _rmsnorm(x, g, out_dtype):
    M, D = x.shape
    tm = _pick(M, (768, 512, 384, 256, 128))
    return pl.pallas_call(
        _rms_kernel,
        grid=(M // tm,),
        in_specs=[pl.BlockSpec((tm, D), lambda i: (i, 0)),
                  pl.BlockSpec((1, D), lambda i: (0, 0))],
        out_specs=pl.BlockSpec((tm, D), lambda i: (i, 0)),
        out_shape=jax.ShapeDtypeStruct((M, D), out_dtype),
    )(x, g.reshape(1, D).astype(F32))


def _norm_mix_kernel(h_ref, g_ref, mu_ref, st_ref, xn_ref, *rest, tm, seq_rows, n_seq, sample_tile):
    outs, xs_sc = rest[:-1], rest[-1]
    i = pl.program_id(0)
    S = SUBLANES

    @pl.when(i == 0)
    def _():
        xs_sc[0:S, :] = jnp.zeros((S, xs_sc.shape[1]), F32)

    x = h_ref[...]
    xn = x * lax.rsqrt(jnp.mean(x * x, axis=-1, keepdims=True) + RMS_EPS) * g_ref[...]
    xn_ref[...] = xn
    xs_sc[S:S + tm, :] = xn
    prev = xs_sc[S - 1:S - 1 + tm, :]
    row = i * tm + lax.broadcasted_iota(jnp.int32, (tm, 1), 0)
    first = row == 0
    for b in range(1, n_seq):
        first = first | (row == b * seq_rows)
    prev = jnp.where(first, 0.0, prev)
    prev = jnp.where(i >= sample_tile, st_ref[...], prev)
    dx = prev - xn
    for s, o_ref in enumerate(outs):
        o_ref[...] = (xn + dx * mu_ref[s:s + 1, :]).astype(o_ref.dtype)
    xs_sc[0:S, :] = xs_sc[tm:tm + S, :]


def _norm_mix(h, g, mu, state, n_seq, seq_rows):
    M, D = h.shape
    MP = n_seq * seq_rows
    tm = M - MP
    n_mix = mu.shape[0]
    assert MP % tm == 0 and tm % SUBLANES == 0 and state.shape == (tm, D)
    row = pl.BlockSpec((tm, D), lambda i: (i, 0))
    full = lambda shape: pl.BlockSpec(shape, lambda i: (0, 0))
    outs = pl.pallas_call(
        functools.partial(_norm_mix_kernel, tm=tm, seq_rows=seq_rows, n_seq=n_seq, sample_tile=MP // tm),
        grid=(M // tm,),
        in_specs=[row, full((1, D)), full((n_mix, D)), full((tm, D))],
        out_specs=[row] * (1 + n_mix),
        out_shape=[jax.ShapeDtypeStruct((M, D), F32)] + [jax.ShapeDtypeStruct((M, D), BF16)] * n_mix,
        scratch_shapes=[pltpu.VMEM((tm + SUBLANES, D), F32)],
        compiler_params=pltpu.CompilerParams(dimension_semantics=("arbitrary",), vmem_limit_bytes=VMEM_LIMIT),
    )(h, g.reshape(1, D), mu, state)
    return outs[0], outs[1:]


def _gate(c, p1, p2, z, cw_ref, cb_ref):
    conv = cb_ref[...] + p2 * cw_ref[0:1, :] + p1 * cw_ref[1:2, :] + c * cw_ref[2:3, :]
    return (jax.nn.silu(conv) * z).astype(BF16)


def _ffn_in_prompt_kernel(x_ref, wc_ref, wz_ref, cw_ref, cb_ref, g_ref, tail_ref, wcb_ref, wzb_ref, cs_ref,
                          *, tm, seq_rows, n_seq, tails):
    i = pl.program_id(1)
    S = SUBLANES

    @pl.when(i == 0)
    def _():
        wcb_ref[...] = wc_ref[...].astype(BF16)
        wzb_ref[...] = wz_ref[...].astype(BF16)
        cs_ref[0:S, :] = jnp.zeros((S, cs_ref.shape[1]), F32)

    x = x_ref[...]
    c = _nn(x, wcb_ref[...])
    z = _nn(x, wzb_ref[...])
    cs_ref[S:S + tm, :] = c
    p1 = cs_ref[S - 1:S - 1 + tm, :]
    p2 = cs_ref[S - 2:S - 2 + tm, :]
    row = i * tm + lax.broadcasted_iota(jnp.int32, (tm, 1), 0)
    t = row
    for b in range(1, n_seq):
        t = jnp.where(row >= b * seq_rows, row - b * seq_rows, t)
    p1 = jnp.where(t >= 1, p1, 0.0)
    p2 = jnp.where(t >= 2, p2, 0.0)
    g_ref[...] = _gate(c, p1, p2, z, cw_ref, cb_ref)
    cs_ref[0:S, :] = cs_ref[tm:tm + S, :]
    for b, (tile, off) in enumerate(tails):
        @pl.when(i == tile)
        def _(b=b, off=off):
            tail_ref[b] = c[off:off + S]


def _ffn_in_sample_kernel(x_ref, wc_ref, wz_ref, cw_ref, cb_ref, b0_ref, b1_ref, alias_ref, g_ref, c_ref):
    del alias_ref
    x = x_ref[...]
    c = _nn(x, wc_ref[...].astype(BF16))
    z = _nn(x, wz_ref[...].astype(BF16))
    c_ref[...] = c
    g_ref[...] = _gate(c, b1_ref[...], b0_ref[...], z, cw_ref, cb_ref)


def _ffn_in(xn, w_in, layer, conv_w, conv_b, buf, n_seq, seq_rows, t_valid):
    M, K = xn.shape
    F = conv_b.shape[0]
    MP = n_seq * seq_rows
    DB = M - MP
    S = SUBLANES
    tn = _pick(F, (512, 256, 128))
    nj = F // tn
    tm = _pick(MP, (640, 512, 384, 256, 128))
    cw = conv_w.astype(F32)
    cb = conv_b.reshape(1, F).astype(F32)
    t0 = t_valid - 2
    assert t0 % S <= S - 2 and MP % DB == 0
    tails = []
    for b in range(n_seq):
        r0 = b * seq_rows + (t0 // S) * S
        assert r0 // tm == (r0 + S - 1) // tm
        tails.append((r0 // tm, r0 % tm))
    wspec = lambda off: pl.BlockSpec((None, K, tn), lambda j, i: (layer, 0, j + off))
    cspec = lambda rows: pl.BlockSpec((rows, tn), lambda j, i: (0, j))
    gated, tail = pl.pallas_call(
        functools.partial(_ffn_in_prompt_kernel, tm=tm, seq_rows=seq_rows, n_seq=n_seq, tails=tuple(tails)),
        grid=(nj, MP // tm),
        in_specs=[pl.BlockSpec((tm, K), lambda j, i: (i, 0)), wspec(0), wspec(nj), cspec(3), cspec(1)],
        out_specs=[pl.BlockSpec((tm, tn), lambda j, i: (i, j)),
                   pl.BlockSpec((n_seq, S, tn), lambda j, i: (0, 0, j))],
        out_shape=[jax.ShapeDtypeStruct((M, F), BF16), jax.ShapeDtypeStruct((n_seq, S, F), F32)],
        scratch_shapes=[pltpu.VMEM((K, tn), BF16), pltpu.VMEM((K, tn), BF16), pltpu.VMEM((tm + S, tn), F32)],
        compiler_params=pltpu.CompilerParams(
            dimension_semantics=("arbitrary", "arbitrary"), vmem_limit_bytes=VMEM_LIMIT),
    )(xn, w_in, w_in, cw, cb)
    wspec1 = lambda off: pl.BlockSpec((None, K, tn), lambda j: (layer, 0, j + off))
    cspec1 = lambda rows: pl.BlockSpec((rows, tn), lambda j: (0, j))
    gated, c_s = pl.pallas_call(
        _ffn_in_sample_kernel,
        grid=(nj,),
        in_specs=[pl.BlockSpec((DB, K), lambda j: (MP // DB, 0)), wspec1(0), wspec1(nj), cspec1(3), cspec1(1),
                  cspec1(DB), cspec1(DB), pl.BlockSpec(memory_space=pl.ANY)],
        out_specs=[pl.BlockSpec((DB, tn), lambda j: (MP // DB, j)), cspec1(DB)],
        out_shape=[jax.ShapeDtypeStruct((M, F), BF16), jax.ShapeDtypeStruct((DB, F), F32)],
        input_output_aliases={7: 0},
        compiler_params=pltpu.CompilerParams(dimension_semantics=("arbitrary",), vmem_limit_bytes=VMEM_LIMIT),
    )(xn, w_in, w_in, cw, cb, buf[:, 0], buf[:, 1], gated)
    off = t0 % S
    return gated, tail[:, off:off + 2], c_s


def _wkv_units(r, lw, k, v, a, b, sts, consts, *, chunk, groups, levels):
    bd, code, tri, eye = consts
    W = MXU_DIM
    units = len(sts)
    hi = lw.astype(BF16)
    rem = lw - hi.astype(F32)
    mid = rem.astype(BF16)
    lo = (rem - mid.astype(F32)).astype(BF16)
    lc = _nn(tri, hi) + _nn(tri, mid) + _nn(tri, lo)
    lc_end = lc[chunk - 1:chunk, :]
    e_neg = jnp.exp(-lc)
    e_end = jnp.exp(lc_end - lc)
    d_end = jnp.exp(lc_end)

    def cut(x):
        return [x[:, u * W:(u + 1) * W] for u in range(units)]

    def each(f, *lists):
        return [f(*xs) for xs in zip(*lists)]

    def stack(x):
        return jnp.concatenate([x] * groups, axis=0) * bd

    def stack_b(x):
        return stack(x).astype(BF16)

    bf = lambda x: x.astype(BF16)
    rt_f = each(stack, cut(r * jnp.exp(lc)))
    rt_s = each(bf, rt_f)
    at_s = each(stack_b, cut(a * jnp.exp(lc - lw)))
    kt_s = each(stack_b, cut(k * e_neg))
    bt_s = each(stack_b, cut(b * e_neg))
    kh_s = each(stack_b, cut(k * e_end))
    bh_s = each(stack_b, cut(b * e_end))
    v_s = each(stack_b, cut(v))
    d_end = cut(d_end)

    strict = (code >= 0) & (code < levels)
    incl = code >= 0
    m_ab = each(_nt, at_s, bt_s)
    m_ak = each(lambda x, y: bf(jnp.where(strict, _nt(x, y), 0.0)), at_s, kt_s)
    n_rb = each(lambda x, y: bf(jnp.where(incl, _nt(x, y), 0.0)), rt_s, bt_s)
    n_rk = each(lambda x, y: bf(jnp.where(incl, _nt(x, y), 0.0)), rt_s, kt_s)
    mv_b = each(lambda x, y: bf(_nn(x, y)), m_ak, v_s)

    inv = each(lambda m: eye + jnp.where(code == 0, m, 0.0), m_ab)
    for lev in range(1, levels):
        inv_b = each(bf, inv)
        off = each(lambda m: bf(jnp.where(code == lev, m, 0.0)), m_ab)
        mid_b = each(lambda o, t: bf(_nn(o, t)), off, inv_b)
        inv = each(lambda t, tb, x: t + _nn(tb, x), inv, inv_b, mid_b)
    inv_b = each(bf, inv)

    p_b = each(lambda t, x: bf(_nn(t, x)), inv_b, at_s)
    q_b = each(lambda t, x: bf(_nn(t, x)), inv_b, mv_b)

    g_b = each(lambda d, x, y: bf(eye * d + _tn(x, y)), d_end, bh_s, p_b)
    f_mat = each(lambda x, y, z, w: _tn(x, y) + _tn(z, w), bh_s, q_b, kh_s, v_s)
    ry_b = each(lambda x, n, p: bf(x + _nn(n, p)), rt_f, n_rb, p_b)
    y_0 = each(lambda n, q, m, w: _nn(n, q) + _nn(m, w), n_rb, q_b, n_rk, v_s)

    st_b = each(bf, sts)
    y_s = each(lambda x, s, y0: _nn(x, s) + y0, ry_b, st_b, y_0)
    new_sts = each(lambda g, s, f: _nn(g, s) + f, g_b, st_b, f_mat)

    def unstack(x):
        y = x[0:chunk]
        for h in range(1, groups):
            y = y + x[h * chunk:(h + 1) * chunk]
        return y

    return jnp.concatenate(each(unstack, y_s), axis=1), new_sts


def _wkv_chunk_kernel(r_ref, lw_ref, k_ref, v_ref, a_ref, b_ref, bd_ref, code_ref, tri_ref, eye_ref,
                      y_ref, s_ref, st_ref, *, chunk, groups, units, t_valid, n_chunks, levels):
    c = pl.program_id(1)

    @pl.when(c == 0)
    def _():
        st_ref[...] = jnp.zeros_like(st_ref)

    row = c * chunk + lax.broadcasted_iota(jnp.int32, (chunk, 1), 0)
    valid = row < t_valid
    consts = (bd_ref[...], code_ref[...], tri_ref[...], eye_ref[...])
    ins = (jnp.where(valid, ref[...], 0.0) for ref in (r_ref, lw_ref, k_ref, v_ref, a_ref, b_ref))
    y, sts = _wkv_units(*ins, [st_ref[u] for u in range(units)], consts, chunk=chunk, groups=groups, levels=levels)
    y_ref[...] = y
    for u in range(units):
        st_ref[u] = sts[u]

    @pl.when(c == n_chunks - 1)
    def _():
        s_ref[0] = st_ref[...]


def _wkv_chunked(r, lw, k, v, a, b, n_seq, seq_rows, t_valid, head):
    M, D = r.shape
    L = WKV_CHUNK
    W = MXU_DIM
    U = WKV_UNITS
    G = W // head
    nu = D // (W * U)
    nc = seq_rows // L
    assert seq_rows % L == 0 and D % (W * U) == 0 and G * L == W
    levels = int(np.log2(L))
    assert 2 ** levels == L
    idx = np.arange(W)
    same = (idx[:, None] // L) == (idx[None, :] // L)
    bd = same.astype(np.float32)
    diff = idx[:, None] ^ idx[None, :]
    code = np.floor(np.log2(np.maximum(diff, 1))).astype(np.int32)
    code = np.where(same & (idx[None, :] < idx[:, None]), code, -1)
    code = np.where(idx[None, :] == idx[:, None], levels, code).astype(np.int32)
    tri = jnp.asarray(np.tril(np.ones((L, L), np.float32)), BF16)
    eye = np.eye(W, dtype=np.float32)

    seq = pl.BlockSpec((L, W * U), lambda p, c: ((p // nu) * nc + c, p % nu))
    const = lambda shape: pl.BlockSpec(shape, lambda p, c: (0, 0))
    y, st = pl.pallas_call(
        functools.partial(_wkv_chunk_kernel, chunk=L, groups=G, units=U, t_valid=t_valid, n_chunks=nc,
                          levels=levels),
        grid=(n_seq * nu, nc),
        in_specs=[seq] * 6 + [const((W, W))] * 2 + [const((L, L)), const((W, W))],
        out_specs=[seq, pl.BlockSpec((1, U, W, W), lambda p, c: (p, 0, 0, 0))],
        out_shape=[jax.ShapeDtypeStruct((M, D), F32),
                   jax.ShapeDtypeStruct((n_seq * nu, U, W, W), F32)],
        scratch_shapes=[pltpu.VMEM((U, W, W), F32)],
        compiler_params=pltpu.CompilerParams(dimension_semantics=("arbitrary", "arbitrary")),
    )(r, lw, k, v, a, b, jnp.asarray(bd), jnp.asarray(code), tri, jnp.asarray(eye))
    ng = nu * U
    st = st.reshape(n_seq, ng, G, head, G, head)
    st = jnp.stack([st[:, :, h, :, h, :] for h in range(G)], axis=2)
    return y, jnp.swapaxes(st, -1, -2).reshape(n_seq, ng * G, head, head)


def _wkv_step_kernel(r_ref, lw_ref, k_ref, v_ref, a_ref, b_ref, s_ref, *rest, heads, head):
    y_ref, so_ref = rest[-2:]
    for h in range(heads):
        w, a, b, k, r, v = (ref[h] for ref in (lw_ref, a_ref, b_ref, k_ref, r_ref, v_ref))
        w = jnp.exp(w)
        ys = []
        for i in range(head):
            s = s_ref[h, i]
            sa = jnp.sum(s * a, axis=0, keepdims=True)
            s_new = s * w + sa * b + v[i:i + 1, :] * k
            so_ref[h, i] = s_new
            ys.append(jnp.sum(s_new * r, axis=0, keepdims=True))
        y_ref[h] = jnp.concatenate(ys, axis=0)


def _wkv_step(r, lw, k, v, a, b, state, layer, prev_out):
    H, N, DB = r.shape
    hb = WKV_STEP_HEADS
    assert H % hb == 0
    vec = pl.BlockSpec((hb, N, DB), lambda g: (g, 0, 0))
    mat = pl.BlockSpec((None, hb, N, N, DB), lambda g: (layer, g, 0, 0, 0))
    in_specs = [vec] * 6 + [mat]
    args = [r, lw, k, v, a, b, state]
    aliases = {}
    if prev_out is not None:
        in_specs.append(pl.BlockSpec(memory_space=pl.ANY))
        args.append(prev_out)
        aliases = {7: 1}
    return pl.pallas_call(
        functools.partial(_wkv_step_kernel, heads=hb, head=N),
        grid=(H // hb,),
        in_specs=in_specs,
        out_specs=[vec, mat],
        out_shape=[jax.ShapeDtypeStruct((H, N, DB), F32), jax.ShapeDtypeStruct(state.shape, F32)],
        input_output_aliases=aliases,
        compiler_params=pltpu.CompilerParams(dimension_semantics=("arbitrary",), vmem_limit_bytes=VMEM_LIMIT),
    )(*args)


def _rwkv_pre_kernel(*refs, head, has_vmix):
    if has_vmix:
        (k_ref, wl_ref, al_ref, v_ref, vf_ref, vl_ref, w0_ref, a0_ref, kk_ref, ka_ref, v0_ref, bd_ref,
         lw_ref, k2_ref, sa_ref, sb_ref, v2_ref) = refs
    else:
        (k_ref, wl_ref, al_ref, w0_ref, a0_ref, kk_ref, ka_ref, bd_ref, lw_ref, k2_ref, sa_ref, sb_ref) = refs
    W = MXU_DIM
    ones_bd = bd_ref[...]
    for c in range(0, k_ref.shape[1], W):
        lanes = slice(c, c + W)
        k = k_ref[:, lanes]
        w_log = -jax.nn.softplus(-(w0_ref[:, lanes] + wl_ref[:, lanes])) - 0.5
        lw_ref[:, lanes] = -jnp.exp(w_log)
        a = jax.nn.sigmoid(a0_ref[:, lanes] + al_ref[:, lanes])
        kk = k * kk_ref[:, lanes]
        kk = kk / jnp.maximum(jnp.sqrt(_head_sum(kk * kk, ones_bd)), 1e-12)
        k2_ref[:, lanes] = k * (1.0 + (a - 1.0) * ka_ref[:, lanes])
        sa_ref[:, lanes] = -kk
        sb_ref[:, lanes] = kk * a
        if has_vmix:
            v = v_ref[:, lanes]
            v2_ref[:, lanes] = v + (vf_ref[:, lanes] - v) * jax.nn.sigmoid(v0_ref[:, lanes] + vl_ref[:, lanes])


def _rwkv_pre(k, wl, al, w0, a0, k_k, k_a, head, vmix=None):
    M, D = k.shape
    W = MXU_DIM
    tm = _pick(M, (256, 128) if vmix is None else (128,))
    idx = np.arange(W)
    ones_bd = jnp.asarray((idx[:, None] // head) == (idx[None, :] // head), BF16)
    row = pl.BlockSpec((tm, D), lambda i: (i, 0))
    vec = pl.BlockSpec((1, D), lambda i: (0, 0))
    bd = pl.BlockSpec((W, W), lambda i: (0, 0))
    as_row = lambda p: p.reshape(1, D)
    if vmix is None:
        args = [k, wl, al, as_row(w0), as_row(a0), as_row(k_k), as_row(k_a), ones_bd]
        in_specs = [row] * 3 + [vec] * 4 + [bd]
        n_out = 4
    else:
        v, v_first, vl, v0 = vmix
        args = [k, wl, al, v, v_first, vl, as_row(w0), as_row(a0), as_row(k_k), as_row(k_a), as_row(v0), ones_bd]
        in_specs = [row] * 6 + [vec] * 5 + [bd]
        n_out = 5
    return pl.pallas_call(
        functools.partial(_rwkv_pre_kernel, head=head, has_vmix=vmix is not None),
        grid=(M // tm,),
        in_specs=in_specs,
        out_specs=[row] * n_out,
        out_shape=[jax.ShapeDtypeStruct((M, D), F32)] * n_out,
        compiler_params=pltpu.CompilerParams(dimension_semantics=("arbitrary",), vmem_limit_bytes=VMEM_LIMIT),
    )(*args)


def _head_sum(x, ones_bd):
    hi = x.astype(BF16)
    lo = (x - hi.astype(F32)).astype(BF16)
    return _nn(hi, ones_bd) + _nn(lo, ones_bd)


def _rwkv_out_kernel(y_ref, r_ref, k_ref, v_ref, g_ref, lw_ref, lb_ref, rk_ref, bd_ref, o_ref, *, head, eps):
    W = MXU_DIM
    ones_bd = bd_ref[...]
    inv_n = 1.0 / head
    for c in range(0, y_ref.shape[1], W):
        lanes = slice(c, c + W)
        y = y_ref[:, lanes]
        d = y - _head_sum(y, ones_bd) * inv_n
        var = _head_sum(d * d, ones_bd) * inv_n
        yn = d * lax.rsqrt(var + eps) * lw_ref[:, lanes] + lb_ref[:, lanes]
        bonus = _head_sum(r_ref[:, lanes] * k_ref[:, lanes] * rk_ref[:, lanes], ones_bd) * v_ref[:, lanes]
        o_ref[:, lanes] = ((yn + bonus) * g_ref[:, lanes]).astype(o_ref.dtype)


def _rwkv_out(y, r, k, v, g, lnx_w, lnx_b, r_k, head):
    M, D = y.shape
    W = MXU_DIM
    tm = _pick(M, (256, 128))
    idx = np.arange(W)
    ones_bd = jnp.asarray((idx[:, None] // head) == (idx[None, :] // head), BF16)
    row = pl.BlockSpec((tm, D), lambda i: (i, 0))
    vec = pl.BlockSpec((1, D), lambda i: (0, 0))
    return pl.pallas_call(
        functools.partial(_rwkv_out_kernel, head=head, eps=head * GN_EPS_PER_CHANNEL),
        grid=(M // tm,),
        in_specs=[row] * 5 + [vec] * 3 + [pl.BlockSpec((W, W), lambda i: (0, 0))],
        out_specs=row,
        out_shape=jax.ShapeDtypeStruct((M, D), BF16),
        compiler_params=pltpu.CompilerParams(dimension_semantics=("arbitrary",), vmem_limit_bytes=VMEM_LIMIT),
    )(y, r, k, v, g, lnx_w.reshape(1, D), lnx_b.reshape(1, D), r_k.reshape(1, D), ones_bd)


def _split_dot(x, w):
    hi = x.astype(BF16)
    lo = (x - hi.astype(F32)).astype(BF16)
    return _nn(hi, w) + _nn(lo, w)


def _qk_prep_kernel(*refs, qk_dim, rope):
    if rope:
        (n_ref, x1_ref, x2_ref, c_ref, s_ref, rs_ref, gn_ref, g1_ref, g2_ref, sn_ref, sr_ref, snt_ref, srt_ref,
         on_ref, o1_ref, o2_ref, ri_ref) = refs
        x1, x2, c, s = x1_ref[...], x2_ref[...], c_ref[...], s_ref[...]
        r1 = x1 * c - x2 * s
        r2 = x2 * c + x1 * s
    else:
        (n_ref, x1_ref, x2_ref, rs_ref, gn_ref, g1_ref, g2_ref, sn_ref, sr_ref, snt_ref, srt_ref,
         on_ref, o1_ref, o2_ref, ri_ref) = refs
        r1, r2 = x1_ref[...], x2_ref[...]
    n = n_ref[...]
    ssq = _split_dot(n * n, sn_ref[...]) + _split_dot(r1 * r1 + r2 * r2, sr_ref[...])
    rinv = lax.rsqrt(ssq * (1.0 / qk_dim) + RMS_EPS)
    ri_ref[...] = rinv
    rinv = rinv * rs_ref[...]
    rn = _split_dot(rinv, snt_ref[...])
    rr = _split_dot(rinv, srt_ref[...])
    on_ref[...] = (n * rn * gn_ref[...]).astype(on_ref.dtype)
    o1_ref[...] = (r1 * rr * g1_ref[...]).astype(o1_ref.dtype)
    o2_ref[...] = (r2 * rr * g2_ref[...]).astype(o2_ref.dtype)


def _qk_prep(src_n, col_n, src_1, col_1, src_2, col_2, heads, nope, half, gain, row_scale, cos_t=None, sin_t=None):
    M = src_n.shape[0]
    wn, wr = heads * nope, heads * half
    tm = _pick(M, (256, 128))
    seg = lambda width, per: jnp.asarray(
        (np.arange(width)[:, None] // per) == np.arange(LANES)[None, :], BF16)
    sn, sr = seg(wn, nope), seg(wr, half)
    gn = jnp.tile(gain[:nope], heads).reshape(1, wn)
    gr = jnp.tile(gain[nope:], heads).reshape(1, wr)
    rows = lambda width, col: pl.BlockSpec((tm, width), lambda i: (i, col))
    full = lambda shape: pl.BlockSpec(shape, lambda i: (0, 0))
    rope = cos_t is not None
    args = [src_n, src_1, src_2] + ([cos_t, sin_t] if rope else []) + [row_scale, gn, gr, gr, sn, sr, sn.T, sr.T]
    in_specs = ([rows(wn, col_n), rows(wr, col_1), rows(wr, col_2)] + ([rows(wr, 0)] * 2 if rope else [])
                + [rows(1, 0), full((1, wn)), full((1, wr)), full((1, wr)),
                   full((wn, LANES)), full((wr, LANES)), full((LANES, wn)), full((LANES, wr))])
    return pl.pallas_call(
        functools.partial(_qk_prep_kernel, qk_dim=nope + 2 * half, rope=rope),
        grid=(M // tm,),
        in_specs=in_specs,
        out_specs=[rows(wn, 0), rows(wr, 0), rows(wr, 0), rows(LANES, 0)],
        out_shape=[jax.ShapeDtypeStruct((M, wn), BF16), jax.ShapeDtypeStruct((M, wr), BF16),
                   jax.ShapeDtypeStruct((M, wr), BF16), jax.ShapeDtypeStruct((M, LANES), F32)],
        compiler_params=pltpu.CompilerParams(dimension_semantics=("arbitrary",), vmem_limit_bytes=VMEM_LIMIT),
    )(*args)


def _flash_kernel(qi_ref, ki_ref, flag_ref, q_ref, k_ref, v_ref, o_ref, m_sc, acc_sc, *, tq, tk, vdim, heads):
    p = pl.program_id(2)
    qi = qi_ref[p]
    ki = ki_ref[p]
    flags = flag_ref[p]
    hs = range(heads)

    @pl.when(ki == 0)
    def _():
        m_sc[...] = jnp.full_like(m_sc, -jnp.inf)
        acc_sc[...] = jnp.zeros_like(acc_sc)

    def update(masked):
        s = [_nt(q_ref[0, h], k_ref[0, h]) for h in hs]
        if masked:
            qpos = qi * tq + lax.broadcasted_iota(jnp.int32, (tq, 1), 0)
            kpos = ki * tk + lax.broadcasted_iota(jnp.int32, (1, tk), 1)
            keep = kpos <= qpos
            s = [jnp.where(keep, x, -jnp.inf) for x in s]
        m_old = [m_sc[h] for h in hs]
        m_new = [jnp.maximum(mo, jnp.max(x, axis=-1, keepdims=True)) for mo, x in zip(m_old, s)]
        pm = [jnp.exp2(x - mn).astype(BF16) for x, mn in zip(s, m_new)]
        for h in hs:
            acc_sc[h] = jnp.exp2(m_old[h] - m_new[h]) * acc_sc[h] + _nn(pm[h], v_ref[0, h])
            m_sc[h] = m_new[h]

    pl.when((flags & 2) != 0)(lambda: update(True))
    pl.when((flags & 2) == 0)(lambda: update(False))

    @pl.when((flags & 1) != 0)
    def _():
        outs = []
        for h in hs:
            acc = acc_sc[h]
            outs.append(acc[:, :vdim] / acc[:, vdim:vdim + 1])
        o_ref[0] = jnp.concatenate(outs, axis=1).astype(o_ref.dtype)


def _flash(q, k, v, vdim, tq, tk):
    B, H, T, E = q.shape
    VA = v.shape[-1]
    HS = FLASH_HEADS
    assert H % HS == 0
    pairs = [(qi, ki) for qi in range(T // tq) for ki in range(T // tk) if ki * tk <= qi * tq + tq - 1]
    n = len(pairs)
    qi_tab = np.array([p[0] for p in pairs], np.int32)
    ki_tab = np.array([p[1] for p in pairs], np.int32)
    flags = np.array([(1 if (i + 1 == n or pairs[i + 1][0] != pairs[i][0]) else 0)
                      + (2 if (ki + 1) * tk - 1 > qi * tq else 0)
                      for i, (qi, ki) in enumerate(pairs)], np.int32)
    grid_spec = pltpu.PrefetchScalarGridSpec(
        num_scalar_prefetch=3,
        grid=(B, H // HS, n),
        in_specs=[pl.BlockSpec((1, HS, tq, E), lambda b, h, p, qt, kt, ft: (b, h, qt[p], 0)),
                  pl.BlockSpec((1, HS, tk, E), lambda b, h, p, qt, kt, ft: (b, h, kt[p], 0)),
                  pl.BlockSpec((1, HS, tk, VA), lambda b, h, p, qt, kt, ft: (b, h, kt[p], 0))],
        out_specs=pl.BlockSpec((1, tq, HS * vdim), lambda b, h, p, qt, kt, ft: (b, qt[p], h)),
        scratch_shapes=[pltpu.VMEM((HS, tq, 1), F32), pltpu.VMEM((HS, tq, VA), F32)],
    )
    return pl.pallas_call(
        functools.partial(_flash_kernel, tq=tq, tk=tk, vdim=vdim, heads=HS),
        grid_spec=grid_spec,
        out_shape=jax.ShapeDtypeStruct((B, T, H * vdim), BF16),
        compiler_params=pltpu.CompilerParams(
            dimension_semantics=("arbitrary", "arbitrary", "arbitrary"), vmem_limit_bytes=VMEM_LIMIT),
    )(jnp.asarray(qi_tab), jnp.asarray(ki_tab), jnp.asarray(flags), q, k, v)


def _paged_kernel(pt_ref, *refs, heads, nope, qk_dim, n_steps, pages, have_rinv):
    del pt_ref
    c_refs, p_refs = refs[:pages], refs[pages:2 * pages]
    qa_ref, qp_ref, x_ref = refs[2 * pages:2 * pages + 3]
    rest = refs[2 * pages + 3:]
    if have_rinv:
        acc_ref, m_ref, l_ref, m_sc, l_sc, acc_sc = rest
    else:
        acc_ref, m_ref, l_ref, rinv_ref, m_sc, l_sc, acc_sc, lhs_sc = rest
    step = pl.program_id(1)
    rows_w = heads * nope

    @pl.when(step == 0)
    def _():
        m_sc[...] = jnp.full_like(m_sc, -jnp.inf)
        l_sc[...] = jnp.zeros_like(l_sc)
        acc_sc[...] = jnp.zeros_like(acc_sc)
        if not have_rinv:
            lhs_sc[rows_w:rows_w + heads, :] = qa_ref[0].astype(BF16)

    if not have_rinv:
        @pl.when((step == 0) & (pl.program_id(0) == 0))
        def _():
            lhs_sc[0:rows_w, :] = x_ref[...]

    c = jnp.concatenate([r[0] for r in c_refs], axis=0).astype(BF16)
    kp = jnp.concatenate([r[0] for r in p_refs], axis=1)
    tokens = c.shape[0]
    if have_rinv:
        rinv = jnp.concatenate([x_ref[0, j] for j in range(x_ref.shape[1])], axis=1)
        s_nope = _nt(qa_ref[0].astype(BF16), c)
    else:
        sub = MXU_DIM
        knts = [_nt(lhs_sc[...], c[t:t + sub]) for t in range(0, tokens, sub)]
        parts = [jnp.sum((knt[:rows_w] * knt[:rows_w]).reshape(heads, nope, sub), axis=1) for knt in knts]
        ssq = jnp.concatenate(parts, axis=1)
        s_nope = jnp.concatenate([knt[rows_w:] for knt in knts], axis=1)
        kss = jnp.sum(kp * kp, axis=0, keepdims=True)
        rinv = lax.rsqrt((ssq + kss) * (1.0 / qk_dim) + RMS_EPS)
        rinv_ref[0, 0] = rinv
    s = (s_nope + _nn(qp_ref[0].astype(BF16), kp.astype(BF16))) * rinv
    m_old = m_sc[...]
    m_new = jnp.maximum(m_old, jnp.max(s, axis=-1, keepdims=True))
    alpha = jnp.exp(m_old - m_new)
    pm = jnp.exp(s - m_new)
    l_sc[...] = alpha * l_sc[...] + jnp.sum(pm, axis=-1, keepdims=True)
    acc_sc[...] = alpha * acc_sc[...] + _nn(pm.astype(BF16), c)
    m_sc[...] = m_new

    @pl.when(step == n_steps - 1)
    def _():
        acc_ref[0] = acc_sc[...]
        m_ref[0] = jnp.broadcast_to(m_sc[...], m_ref.shape[1:])
        l_ref[0] = jnp.broadcast_to(l_sc[...], l_ref.shape[1:])


def _paged_attn(page_table, cache_ckv, cache_kpe_t, qa, qp, nope, *, wnt=None, rinv=None):
    DB, n_pages = page_table.shape
    _, page, R = cache_ckv.shape
    rope = cache_kpe_t.shape[1]
    H = qa.shape[1]
    have_rinv = rinv is not None
    P = PAGES_PER_STEP * (PAGE_STEP_RATIO if have_rinv else 1)
    assert n_pages % P == 0
    n_steps = n_pages // P
    page_spec = lambda shape, i: pl.BlockSpec((1,) + shape, lambda b, s, pt: (pt[b, P * s + i], 0, 0))
    if have_rinv:
        rinv_spec = pl.BlockSpec((1, PAGE_STEP_RATIO, H, PAGES_PER_STEP * page), lambda b, s, pt: (b, s, 0, 0))
    else:
        rinv_spec = pl.BlockSpec((1, 1, H, P * page), lambda b, s, pt: (b, s, 0, 0))
    x_spec = rinv_spec if have_rinv else pl.BlockSpec(wnt.shape, lambda b, s, pt: (0, 0))
    stat_spec = pl.BlockSpec((1, H, LANES), lambda b, s, pt: (b, 0, 0))
    out_specs = [pl.BlockSpec((1, H, R), lambda b, s, pt: (b, 0, 0)), stat_spec, stat_spec]
    out_shape = [jax.ShapeDtypeStruct((DB, H, R), F32),
                 jax.ShapeDtypeStruct((DB, H, LANES), F32),
                 jax.ShapeDtypeStruct((DB, H, LANES), F32)]
    scratch = [pltpu.VMEM((H, 1), F32), pltpu.VMEM((H, 1), F32), pltpu.VMEM((H, R), F32)]
    if not have_rinv:
        out_specs.append(rinv_spec)
        out_shape.append(jax.ShapeDtypeStruct((DB, n_steps, H, P * page), F32))
        scratch.append(pltpu.VMEM((H * nope + H, R), BF16))
    grid_spec = pltpu.PrefetchScalarGridSpec(
        num_scalar_prefetch=1,
        grid=(DB, n_steps),
        in_specs=([page_spec((page, R), i) for i in range(P)] + [page_spec((rope, page), i) for i in range(P)]
                  + [pl.BlockSpec((1, H, R), lambda b, s, pt: (b, 0, 0)),
                     pl.BlockSpec((1, H, rope), lambda b, s, pt: (b, 0, 0)), x_spec]),
        out_specs=out_specs,
        scratch_shapes=scratch,
    )
    outs = pl.pallas_call(
        functools.partial(_paged_kernel, heads=H, nope=nope, qk_dim=nope + rope, n_steps=n_steps, pages=P,
                          have_rinv=have_rinv),
        grid_spec=grid_spec,
        out_shape=out_shape,
        compiler_params=pltpu.CompilerParams(
            dimension_semantics=("arbitrary", "arbitrary"), vmem_limit_bytes=VMEM_LIMIT),
    )(page_table, *([cache_ckv] * P), *([cache_kpe_t] * P), qa, qp, rinv if have_rinv else wnt)
    acc, m, l = outs[:3]
    return acc, m[:, :, 0], l[:, :, 0], (rinv if have_rinv else outs[3])


def _pad_cols(w, n):
    return jnp.pad(w, ((0, 0), (0, n - w.shape[1])))


def _pad_rows(w, n):
    return jnp.pad(w, ((0, n - w.shape[0]), (0, 0)))


def _lora(x, w1, w2, act):
    rank = w1.shape[1]
    rp = -(-rank // LANES) * LANES
    mid = _mm(x, _pad_cols(w1, rp), act=act, out_dtype=BF16)
    return _mm(mid, _pad_rows(w2, rp))


def _rope(x, cos, sin):
    half = x.shape[-1] // 2
    x1, x2 = x[..., :half], x[..., half:]
    return jnp.concatenate([x1 * cos - x2 * sin, x2 * cos + x1 * sin], axis=-1)


def kernel(x_prompt, x_sample, state_shift, state_wkv, state_conv, cache_ckv, cache_kpe, page_table, meta_tokens, norm_mix, norm_ffn, mu, w_rkv, w_o_a, w0, w1, w2, a0, a1, a2, v0, v1, v2, g1, g2, k_k, k_a, r_k, lnx_w, lnx_b, ffn_w_in, ffn_conv_w, ffn_conv_b, ffn_w_out, norm_kv, w_dkv, g_ckv, w_ukv, g_k, w_dq, g_q, w_uq, g_qn, w_o_b):
    B, seq, D = x_prompt.shape
    DB = x_sample.shape[0]
    assert x_sample.shape[1] == 1
    n_meta = meta_tokens.shape[0]
    depth = norm_mix.shape[0]
    n_a = mu.shape[0]
    HA, NA = r_k.shape[1], r_k.shape[2]
    assert ffn_conv_w.shape[1] == 3
    R = g_ckv.shape[0]
    rope = w_dkv.shape[1] - R
    HB = w_ukv.shape[1]
    nope = g_k.shape[0] - rope // 2
    vdim = w_ukv.shape[2] - nope
    qk = nope + rope
    T = seq + n_meta
    Tp = -(-T // ROW_ALIGN) * ROW_ALIGN
    Tpa = -(-Tp // ATTN_ALIGN) * ATTN_ALIGN
    MP = B * Tp
    M = MP + DB
    past_len = page_table.shape[1] * cache_ckv.shape[1]
    scale = qk ** -0.5

    h0 = jnp.concatenate([jnp.broadcast_to(meta_tokens[None], (B, n_meta, D)), x_prompt], axis=1)
    h0 = jnp.pad(h0, ((0, 0), (0, Tp - T), (0, 0)))
    h = jnp.concatenate([h0.reshape(MP, D), x_sample.reshape(DB, D)], axis=0)

    t_of_row = jnp.concatenate([jnp.tile(jnp.arange(Tp), B), jnp.full((DB,), past_len)])
    inv_freq = ROPE_THETA ** (-jnp.arange(0, rope, 2, dtype=F32) / rope)
    ang = t_of_row.astype(F32)[:, None] * inv_freq[None]
    cos, sin = jnp.cos(ang), jnp.sin(ang)
    last_rows = np.array([b * Tp + T - 1 for b in range(B)])

    state_wkv_t = jnp.transpose(state_wkv, (0, 2, 3, 4, 1))
    cache_kpe_t = jnp.swapaxes(cache_kpe, 1, 2)
    w_rkv3 = w_rkv.reshape(n_a * 3, D, D)
    half = rope // 2
    wn, wr = HB * nope, HB * half
    uq_part = lambda lo, hi: w_uq[..., lo:hi].reshape(w_uq.shape[0], w_uq.shape[1], HB * (hi - lo))
    w_uq3 = jnp.concatenate([uq_part(0, nope), uq_part(nope, nope + half), uq_part(nope + half, qk)], axis=-1)
    cos_t, sin_t = jnp.tile(cos, (1, HB)), jnp.tile(sin, (1, HB))
    q_row_scale = jnp.concatenate([jnp.full((MP, 1), scale * LOG2E, F32), jnp.full((DB, 1), scale, F32)])

    def to_heads(parts):
        t = jnp.concatenate([p[:MP].reshape(B, Tp, HB, -1) for p in parts], axis=-1)
        return jnp.pad(t.transpose(0, 2, 1, 3), ((0, 0), (0, 0), (0, Tpa - Tp), (0, 0)))
    w_ob3 = w_o_b.reshape(w_o_b.shape[0], HB * vdim, D)

    shift_p, shift_s, wkv_p, conv_p, conv_s = [], [], [], [], []
    wkv_s_t = None
    v_first = None
    kv_p = kv_s = None
    ckv = kpe = None

    for i in range(depth):
        if i < n_a:
            xn, xs = _norm_mix(h, norm_mix[i], mu[i], state_shift[i], B, Tp)
            shift_p.append(xn[last_rows])
            shift_s.append(xn[MP:])
            r = _mm(xs[0], w_rkv3, layer=3 * i)
            k = _mm(xs[1], w_rkv3, layer=3 * i + 1)
            v = _mm(xs[2], w_rkv3, layer=3 * i + 2)
            wl = _lora(xs[3], w1[i], w2[i], jnp.tanh)
            al = _lora(xs[4], a1[i], a2[i], None)
            g = _lora(xs[5], g1[i], g2[i], jax.nn.sigmoid)
            if i > 0:
                vl = _lora(xs[2], v1[i - 1], v2[i - 1], None)
                log_decay, k, sa, sb, v = _rwkv_pre(k, wl, al, w0[i], a0[i], k_k[i], k_a[i], NA,
                                                    vmix=(v, v_first, vl, v0[i - 1]))
            else:
                v_first = v
                log_decay, k, sa, sb = _rwkv_pre(k, wl, al, w0[i], a0[i], k_k[i], k_a[i], NA)
            y, st_p = _wkv_chunked(r, log_decay, k, v, sa, sb, B, Tp, T, NA)
            lanes_t = lambda t: t[MP:].reshape(DB, HA, NA).transpose(1, 2, 0)
            y_s, wkv_s_t = _wkv_step(*(lanes_t(t) for t in (r, log_decay, k, v, sa, sb)), state_wkv_t, i, wkv_s_t)
            y = lax.dynamic_update_slice(y, y_s.transpose(2, 0, 1).reshape(DB, D), (MP, 0))
            wkv_p.append(st_p)
            gated = _rwkv_out(y, r, k, v, g, lnx_w[i], lnx_b[i], r_k[i], NA)
            h = _mm(gated, w_o_a, layer=i, residual=h)
        else:
            j = i - n_a
            xn = _rmsnorm(h, norm_mix[i], BF16)
            cq = _rmsnorm(_mm(xn, w_dq, layer=j), g_q[j], BF16)
            q2d = _mm(cq, w_uq3, layer=j)
            q_parts = _qk_prep(q2d, 0, q2d, wn // wr, q2d, wn // wr + 1, HB, nope, half, g_qn[j], q_row_scale,
                               cos_t, sin_t)[:3]
            o_p = _flash(to_heads(q_parts), kv_p[0], kv_p[1], vdim,
                         tq=Tpa // FLASH_BLOCKS, tk=Tpa // FLASH_BLOCKS)[:, :Tp]
            qn_s, q1_s, q2_s = (t[MP:].astype(F32).reshape(DB, HB, -1) for t in q_parts)
            qn = qn_s * g_k[:nope]
            qp = jnp.concatenate([q1_s * g_k[nope:], q2_s * g_k[nope:]], axis=-1)
            qa = _bmm(qn.transpose(1, 0, 2), kv_s['wnt3']).transpose(1, 0, 2)
            acc, m, l, kv_s['rinv_cache'] = _paged_attn(
                page_table, cache_ckv, cache_kpe_t, qa, qp, nope, wnt=kv_s['wnt'], rinv=kv_s['rinv_cache'])
            c_new, kp_new = kv_s['ckv'], kv_s['kpe']
            s_new = kv_s['rinv'] * (jnp.sum(qa * c_new[:, None, :], axis=-1) + jnp.sum(qp * kp_new[:, None, :], axis=-1))
            m_f = jnp.maximum(m, s_new)
            alpha = jnp.exp(m - m_f)
            pn = jnp.exp(s_new - m_f)
            l_f = l * alpha + pn
            ctx = (acc * alpha[..., None] + pn[..., None] * c_new[:, None, :]) / l_f[..., None]
            o_s = _bmm(ctx.transpose(1, 0, 2), kv_s['wv']).transpose(1, 0, 2).reshape(DB, HB * vdim)
            attn = jnp.concatenate([o_p.reshape(MP, HB * vdim), o_s.astype(BF16)], axis=0)
            h = _mm(attn, w_ob3, layer=j, residual=h)

        xn = _rmsnorm(h, norm_ffn[i], BF16)
        gated, c_tail, c_s = _ffn_in(xn, ffn_w_in, i, ffn_conv_w[i], ffn_conv_b[i], state_conv[i], B, Tp, T)
        conv_p.append(c_tail)
        conv_s.append(jnp.stack([state_conv[i][:, 1], c_s], axis=1))
        h = _mm(gated, ffn_w_out, layer=i, residual=h)

        if i == n_a - 1:
            xk = _rmsnorm(h, norm_kv, BF16)
            ckv = _rmsnorm(_mm(xk, w_dkv[:, :R]), g_ckv, F32)
            kpe = _rope(_mm(xk, w_dkv[:, R:]), cos, sin)
            w_ukv2 = jnp.concatenate([w_ukv[:, :, :nope].reshape(R, wn), w_ukv[:, :, nope:].reshape(R, HB * vdim)],
                                     axis=1)
            kv2d = _mm(ckv.astype(BF16), w_ukv2)
            kp1 = jnp.tile(kpe[:, :half], (1, HB))
            kp2 = jnp.tile(kpe[:, half:], (1, HB))
            *k_parts, rinv = _qk_prep(kv2d, 0, kp1, 0, kp2, 0, HB, nope, half, g_k, jnp.ones((M, 1), F32))
            rinv = rinv[:, :HB]
            v3 = kv2d[:MP, wn:].reshape(MP, HB, vdim).astype(BF16)
            v_aug = jnp.concatenate([v3, jnp.ones((MP, HB, 1), BF16), jnp.zeros((MP, HB, vdim - 1), BF16)], axis=-1)
            kv_p = (to_heads(k_parts), to_heads([v_aug.reshape(MP, HB * 2 * vdim)]))
            wnt3 = w_ukv[:, :, :nope].transpose(1, 2, 0)
            kv_s = dict(ckv=ckv[MP:], kpe=kpe[MP:], rinv=rinv[MP:], wnt3=wnt3, rinv_cache=None,
                        wnt=wnt3.reshape(HB * nope, R).astype(BF16),
                        wv=w_ukv[:, :, nope:].transpose(1, 0, 2))

    wkv_s = jnp.transpose(wkv_s_t, (0, 4, 1, 2, 3))
    return (h[:MP].reshape(B, Tp, D)[:, n_meta:T], h[MP:].reshape(DB, 1, D),
            jnp.stack(shift_p), jnp.stack(wkv_p), jnp.stack(conv_p),
            ckv[:MP].reshape(B, Tp, R)[:, :T], kpe[:MP].reshape(B, Tp, rope)[:, :T],
            jnp.stack(shift_s), wkv_s, jnp.stack(conv_s),
            ckv[MP:].reshape(DB, 1, R), kpe[MP:].reshape(DB, 1, rope))
```

```python
import functools

import numpy as np
import jax
import jax.numpy as jnp
from jax import lax
from jax.experimental import pallas as pl
from jax.experimental.pallas import tpu as pltpu

F32 = jnp.float32
BF16 = jnp.bfloat16

RMS_EPS = 1e-6
ROPE_THETA = 10000.0
GN_EPS_PER_CHANNEL = 1e-5
LOG2E = 1.4426950408889634
LANES = 128
SUBLANES = 8
MXU_DIM = 256
WKV_CHUNK = 64
WKV_UNITS = 8
WKV_STEP_HEADS = 2
ROW_ALIGN = 64
ATTN_ALIGN = 128
FLASH_BLOCKS = 3
FLASH_HEADS = 2
PAGES_PER_STEP = 32
PAGE_STEP_RATIO = 1
VMEM_LIMIT = 56 * 1024 * 1024


def _nt(a, b):
    return lax.dot_general(a, b, (((1,), (1,)), ((), ())), preferred_element_type=F32)


def _tn(a, b):
    return lax.dot_general(a, b, (((0,), (0,)), ((), ())), preferred_element_type=F32)


def _nn(a, b):
    return jnp.dot(a, b, preferred_element_type=F32)


def _pick(n, candidates):
    for c in candidates:
        if n % c == 0:
            return c
    return n


def _mm_kernel(x_ref, w_ref, *rest, act, has_res):
    if has_res:
        r_ref, o_ref, wb_ref = rest
    else:
        o_ref, wb_ref = rest

    @pl.when(pl.program_id(1) == 0)
    def _():
        wb_ref[...] = w_ref[...].astype(BF16)

    acc = _nn(x_ref[...].astype(BF16), wb_ref[...])
    if act is not None:
        acc = act(acc)
    if has_res:
        acc = acc + r_ref[...]
    o_ref[...] = acc.astype(o_ref.dtype)


def _mm(x, w, *, layer=None, act=None, residual=None, out_dtype=F32):
    M, K = x.shape
    N = w.shape[-1]
    tm = _pick(M, (768, 512, 384, 256, 128))
    if K > 4096:
        tm = _pick(M, (384, 256, 128))
    tn = _pick(N, (1024, 512, 256, 128)) if K <= 2048 else _pick(N, (512, 256, 128))
    grid = (N // tn, M // tm)
    if layer is None:
        w_spec = pl.BlockSpec((K, tn), lambda j, i: (0, j))
    else:
        w_spec = pl.BlockSpec((None, K, tn), lambda j, i: (layer, 0, j))
    in_specs = [pl.BlockSpec((tm, K), lambda j, i: (i, 0)), w_spec]
    args = [x, w]
    if residual is not None:
        in_specs.append(pl.BlockSpec((tm, tn), lambda j, i: (i, j)))
        args.append(residual)
    return pl.pallas_call(
        functools.partial(_mm_kernel, act=act, has_res=residual is not None),
        grid=grid,
        in_specs=in_specs,
        out_specs=pl.BlockSpec((tm, tn), lambda j, i: (i, j)),
        out_shape=jax.ShapeDtypeStruct((M, N), out_dtype),
        scratch_shapes=[pltpu.VMEM((K, tn), BF16)],
        compiler_params=pltpu.CompilerParams(
            dimension_semantics=("arbitrary", "arbitrary"), vmem_limit_bytes=VMEM_LIMIT),
    )(*args)


def _bmm_kernel(x_ref, w_ref, o_ref):
    o_ref[0] = _nn(x_ref[0].astype(BF16), w_ref[0].astype(BF16))


def _bmm(x, w):
    G, M, K = x.shape
    N = w.shape[2]
    return pl.pallas_call(
        _bmm_kernel,
        grid=(G,),
        in_specs=[pl.BlockSpec((1, M, K), lambda g: (g, 0, 0)),
                  pl.BlockSpec((1, K, N), lambda g: (g, 0, 0))],
        out_specs=pl.BlockSpec((1, M, N), lambda g: (g, 0, 0)),
        out_shape=jax.ShapeDtypeStruct((G, M, N), F32),
    )(x, w)


def _rms_kernel(x_ref, g_ref, o_ref):
    x = x_ref[...].astype(F32)
    y = x * lax.rsqrt(jnp.mean(x * x, axis=-1, keepdims=True) + RMS_EPS)
    o_ref[...] = (y * g_ref[...]).astype(o_ref.dtype)


def _rmsnorm(x, g, out_dtype):
    M, D = x.shape
    tm = _pick(M, (768, 512, 384, 256, 128))
    return pl.pallas_call(
        _rms_kernel,
        grid=(M // tm,),
        in_specs=[pl.BlockSpec((tm, D), lambda i: (i, 0)),
                  pl.BlockSpec((1, D), lambda i: (0, 0))],
        out_specs=pl.BlockSpec((tm, D), lambda i: (i, 0)),
        out_shape=jax.ShapeDtypeStruct((M, D), out_dtype),
    )(x, g.reshape(1, D).astype(F32))


def _norm_mix_kernel(h_ref, g_ref, mu_ref, st_ref, xn_ref, *rest, tm, seq_rows, n_seq, sample_tile):
    outs, xs_sc = rest[:-1], rest[-1]
    i = pl.program_id(0)
    S = SUBLANES

    @pl.when(i == 0)
    def _():
        xs_sc[0:S, :] = jnp.zeros((S, xs_sc.shape[1]), F32)

    x = h_ref[...]
    xn = x * lax.rsqrt(jnp.mean(x * x, axis=-1, keepdims=True) + RMS_EPS) * g_ref[...]
    xn_ref[...] = xn
    xs_sc[S:S + tm, :] = xn
    prev = xs_sc[S - 1:S - 1 + tm, :]
    row = i * tm + lax.broadcasted_iota(jnp.int32, (tm, 1), 0)
    first = row == 0
    for b in range(1, n_seq):
        first = first | (row == b * seq_rows)
    prev = jnp.where(first, 0.0, prev)
    prev = jnp.where(i >= sample_tile, st_ref[...], prev)
    dx = prev - xn
    for s, o_ref in enumerate(outs):
        o_ref[...] = (xn + dx * mu_ref[s:s + 1, :]).astype(o_ref.dtype)
    xs_sc[0:S, :] = xs_sc[tm:tm + S, :]


def _norm_mix(h, g, mu, state, n_seq, seq_rows):
    M, D = h.shape
    MP = n_seq * seq_rows
    tm = M - MP
    n_mix = mu.shape[0]
    assert MP % tm == 0 and tm % SUBLANES == 0 and state.shape == (tm, D)
    row = pl.BlockSpec((tm, D), lambda i: (i, 0))
    full = lambda shape: pl.BlockSpec(shape, lambda i: (0, 0))
    outs = pl.pallas_call(
        functools.partial(_norm_mix_kernel, tm=tm, seq_rows=seq_rows, n_seq=n_seq, sample_tile=MP // tm),
        grid=(M // tm,),
        in_specs=[row, full((1, D)), full((n_mix, D)), full((tm, D))],
        out_specs=[row] * (1 + n_mix),
        out_shape=[jax.ShapeDtypeStruct((M, D), F32)] + [jax.ShapeDtypeStruct((M, D), BF16)] * n_mix,
        scratch_shapes=[pltpu.VMEM((tm + SUBLANES, D), F32)],
        compiler_params=pltpu.CompilerParams(dimension_semantics=("arbitrary",), vmem_limit_bytes=VMEM_LIMIT),
    )(h, g.reshape(1, D), mu, state)
    return outs[0], outs[1:]


def _gate(c, p1, p2, z, cw_ref, cb_ref):
    conv = cb_ref[...] + p2 * cw_ref[0:1, :] + p1 * cw_ref[1:2, :] + c * cw_ref[2:3, :]
    return (jax.nn.silu(conv) * z).astype(BF16)


def _ffn_in_prompt_kernel(x_ref, wc_ref, wz_ref, cw_ref, cb_ref, g_ref, tail_ref, wcb_ref, wzb_ref, cs_ref,
                          *, tm, seq_rows, n_seq, tails):
    i = pl.program_id(1)
    S = SUBLANES

    @pl.when(i == 0)
    def _():
        wcb_ref[...] = wc_ref[...].astype(BF16)
        wzb_ref[...] = wz_ref[...].astype(BF16)
        cs_ref[0:S, :] = jnp.zeros((S, cs_ref.shape[1]), F32)

    x = x_ref[...]
    c = _nn(x, wcb_ref[...])
    z = _nn(x, wzb_ref[...])
    cs_ref[S:S + tm, :] = c
    p1 = cs_ref[S - 1:S - 1 + tm, :]
    p2 = cs_ref[S - 2:S - 2 + tm, :]
    row = i * tm + lax.broadcasted_iota(jnp.int32, (tm, 1), 0)
    t = row
    for b in range(1, n_seq):
        t = jnp.where(row >= b * seq_rows, row - b * seq_rows, t)
    p1 = jnp.where(t >= 1, p1, 0.0)
    p2 = jnp.where(t >= 2, p2, 0.0)
    g_ref[...] = _gate(c, p1, p2, z, cw_ref, cb_ref)
    cs_ref[0:S, :] = cs_ref[tm:tm + S, :]
    for b, (tile, off) in enumerate(tails):
        @pl.when(i == tile)
        def _(b=b, off=off):
            tail_ref[b] = c[off:off + S]


def _ffn_in_sample_kernel(x_ref, wc_ref, wz_ref, cw_ref, cb_ref, b0_ref, b1_ref, alias_ref, g_ref, c_ref):
    del alias_ref
    x = x_ref[...]
    c = _nn(x, wc_ref[...].astype(BF16))
    z = _nn(x, wz_ref[...].astype(BF16))
    c_ref[...] = c
    g_ref[...] = _gate(c, b1_ref[...], b0_ref[...], z, cw_ref, cb_ref)


def _ffn_in(xn, w_in, layer, conv_w, conv_b, buf, n_seq, seq_rows, t_valid):
    M, K = xn.shape
    F = conv_b.shape[0]
    MP = n_seq * seq_rows
    DB = M - MP
    S = SUBLANES
    tn = _pick(F, (512, 256, 128))
    nj = F // tn
    tm = _pick(MP, (640, 512, 384, 256, 128))
    cw = conv_w.astype(F32)
    cb = conv_b.reshape(1, F).astype(F32)
    t0 = t_valid - 2
    assert t0 % S <= S - 2 and MP % DB == 0
    tails = []
    for b in range(n_seq):
        r0 = b * seq_rows + (t0 // S) * S
        assert r0 // tm == (r0 + S - 1) // tm
        tails.append((r0 // tm, r0 % tm))
    wspec = lambda off: pl.BlockSpec((None, K, tn), lambda j, i: (layer, 0, j + off))
    cspec = lambda rows: pl.BlockSpec((rows, tn), lambda j, i: (0, j))
    gated, tail = pl.pallas_call(
        functools.partial(_ffn_in_prompt_kernel, tm=tm, seq_rows=seq_rows, n_seq=n_seq, tails=tuple(tails)),
        grid=(nj, MP // tm),
        in_specs=[pl.BlockSpec((tm, K), lambda j, i: (i, 0)), wspec(0), wspec(nj), cspec(3), cspec(1)],
        out_specs=[pl.BlockSpec((tm, tn), lambda j, i: (i, j)),
                   pl.BlockSpec((n_seq, S, tn), lambda j, i: (0, 0, j))],
        out_shape=[jax.ShapeDtypeStruct((M, F), BF16), jax.ShapeDtypeStruct((n_seq, S, F), F32)],
        scratch_shapes=[pltpu.VMEM((K, tn), BF16), pltpu.VMEM((K, tn), BF16), pltpu.VMEM((tm + S, tn), F32)],
        compiler_params=pltpu.CompilerParams(
            dimension_semantics=("arbitrary", "arbitrary"), vmem_limit_bytes=VMEM_LIMIT),
    )(xn, w_in, w_in, cw, cb)
    wspec1 = lambda off: pl.BlockSpec((None, K, tn), lambda j: (layer, 0, j + off))
    cspec1 = lambda rows: pl.BlockSpec((rows, tn), lambda j: (0, j))
    gated, c_s = pl.pallas_call(
        _ffn_in_sample_kernel,
        grid=(nj,),
        in_specs=[pl.BlockSpec((DB, K), lambda j: (MP // DB, 0)), wspec1(0), wspec1(nj), cspec1(3), cspec1(1),
                  cspec1(DB), cspec1(DB), pl.BlockSpec(memory_space=pl.ANY)],
        out_specs=[pl.BlockSpec((DB, tn), lambda j: (MP // DB, j)), cspec1(DB)],
        out_shape=[jax.ShapeDtypeStruct((M, F), BF16), jax.ShapeDtypeStruct((DB, F), F32)],
        input_output_aliases={7: 0},
        compiler_params=pltpu.CompilerParams(dimension_semantics=("arbitrary",), vmem_limit_bytes=VMEM_LIMIT),
    )(xn, w_in, w_in, cw, cb, buf[:, 0], buf[:, 1], gated)
    off = t0 % S
    return gated, tail[:, off:off + 2], c_s


def _wkv_units(r, lw, k, v, a, b, sts, consts, *, chunk, groups, levels):
    bd, code, tri, eye = consts
    W = MXU_DIM
    units = len(sts)
    hi = lw.astype(BF16)
    rem = lw - hi.astype(F32)
    mid = rem.astype(BF16)
    lo = (rem - mid.astype(F32)).astype(BF16)
    lc = _nn(tri, hi) + _nn(tri, mid) + _nn(tri, lo)
    lc_end = lc[chunk - 1:chunk, :]
    e_neg = jnp.exp(-lc)
    e_end = jnp.exp(lc_end - lc)
    d_end = jnp.exp(lc_end)

    def cut(x):
        return [x[:, u * W:(u + 1) * W] for u in range(units)]

    def each(f, *lists):
        return [f(*xs) for xs in zip(*lists)]

    def stack(x):
        return jnp.concatenate([x] * groups, axis=0) * bd

    def stack_b(x):
        return stack(x).astype(BF16)

    bf = lambda x: x.astype(BF16)
    rt_f = each(stack, cut(r * jnp.exp(lc)))
    rt_s = each(bf, rt_f)
    at_s = each(stack_b, cut(a * jnp.exp(lc - lw)))
    kt_s = each(stack_b, cut(k * e_neg))
    bt_s = each(stack_b, cut(b * e_neg))
    kh_s = each(stack_b, cut(k * e_end))
    bh_s = each(stack_b, cut(b * e_end))
    v_s = each(stack_b, cut(v))
    d_end = cut(d_end)

    strict = (code >= 0) & (code < levels)
    incl = code >= 0
    m_ab = each(_nt, at_s, bt_s)
    m_ak = each(lambda x, y: bf(jnp.where(strict, _nt(x, y), 0.0)), at_s, kt_s)
    n_rb = each(lambda x, y: bf(jnp.where(incl, _nt(x, y), 0.0)), rt_s, bt_s)
    n_rk = each(lambda x, y: bf(jnp.where(incl, _nt(x, y), 0.0)), rt_s, kt_s)
    mv_b = each(lambda x, y: bf(_nn(x, y)), m_ak, v_s)

    inv = each(lambda m: eye + jnp.where(code == 0, m, 0.0), m_ab)
    for lev in range(1, levels):
        inv_b = each(bf, inv)
        off = each(lambda m: bf(jnp.where(code == lev, m, 0.0)), m_ab)
        mid_b = each(lambda o, t: bf(_nn(o, t)), off, inv_b)
        inv = each(lambda t, tb, x: t + _nn(tb, x), inv, inv_b, mid_b)
    inv_b = each(bf, inv)

    p_b = each(lambda t, x: bf(_nn(t, x)), inv_b, at_s)
    q_b = each(lambda t, x: bf(_nn(t, x)), inv_b, mv_b)

    g_b = each(lambda d, x, y: bf(eye * d + _tn(x, y)), d_end, bh_s, p_b)
    f_mat = each(lambda x, y, z, w: _tn(x, y) + _tn(z, w), bh_s, q_b, kh_s, v_s)
    ry_b = each(lambda x, n, p: bf(x + _nn(n, p)), rt_f, n_rb, p_b)
    y_0 = each(lambda n, q, m, w: _nn(n, q) + _nn(m, w), n_rb, q_b, n_rk, v_s)

    st_b = each(bf, sts)
    y_s = each(lambda x, s, y0: _nn(x, s) + y0, ry_b, st_b, y_0)
    new_sts = each(lambda g, s, f: _nn(g, s) + f, g_b, st_b, f_mat)

    def unstack(x):
        y = x[0:chunk]
        for h in range(1, groups):
            y = y + x[h * chunk:(h + 1) * chunk]
        return y

    return jnp.concatenate(each(unstack, y_s), axis=1), new_sts


def _wkv_chunk_kernel(r_ref, lw_ref, k_ref, v_ref, a_ref, b_ref, bd_ref, code_ref, tri_ref, eye_ref,
                      y_ref, s_ref, st_ref, *, chunk, groups, units, t_valid, n_chunks, levels):
    c = pl.program_id(1)

    @pl.when(c == 0)
    def _():
        st_ref[...] = jnp.zeros_like(st_ref)

    row = c * chunk + lax.broadcasted_iota(jnp.int32, (chunk, 1), 0)
    valid = row < t_valid
    consts = (bd_ref[...], code_ref[...], tri_ref[...], eye_ref[...])
    ins = (jnp.where(valid, ref[...], 0.0) for ref in (r_ref, lw_ref, k_ref, v_ref, a_ref, b_ref))
    y, sts = _wkv_units(*ins, [st_ref[u] for u in range(units)], consts, chunk=chunk, groups=groups, levels=levels)
    y_ref[...] = y
    for u in range(units):
        st_ref[u] = sts[u]

    @pl.when(c == n_chunks - 1)
    def _():
        s_ref[0] = st_ref[...]


def _wkv_chunked(r, lw, k, v, a, b, n_seq, seq_rows, t_valid, head):
    M, D = r.shape
    L = WKV_CHUNK
    W = MXU_DIM
    U = WKV_UNITS
    G = W // head
    nu = D // (W * U)
    nc = seq_rows // L
    assert seq_rows % L == 0 and D % (W * U) == 0 and G * L == W
    levels = int(np.log2(L))
    assert 2 ** levels == L
    idx = np.arange(W)
    same = (idx[:, None] // L) == (idx[None, :] // L)
    bd = same.astype(np.float32)
    diff = idx[:, None] ^ idx[None, :]
    code = np.floor(np.log2(np.maximum(diff, 1))).astype(np.int32)
    code = np.where(same & (idx[None, :] < idx[:, None]), code, -1)
    code = np.where(idx[None, :] == idx[:, None], levels, code).astype(np.int32)
    tri = jnp.asarray(np.tril(np.ones((L, L), np.float32)), BF16)
    eye = np.eye(W, dtype=np.float32)

    seq = pl.BlockSpec((L, W * U), lambda p, c: ((p // nu) * nc + c, p % nu))
    const = lambda shape: pl.BlockSpec(shape, lambda p, c: (0, 0))
    y, st = pl.pallas_call(
        functools.partial(_wkv_chunk_kernel, chunk=L, groups=G, units=U, t_valid=t_valid, n_chunks=nc,
                          levels=levels),
        grid=(n_seq * nu, nc),
        in_specs=[seq] * 6 + [const((W, W))] * 2 + [const((L, L)), const((W, W))],
        out_specs=[seq, pl.BlockSpec((1, U, W, W), lambda p, c: (p, 0, 0, 0))],
        out_shape=[jax.ShapeDtypeStruct((M, D), F32),
                   jax.ShapeDtypeStruct((n_seq * nu, U, W, W), F32)],
        scratch_shapes=[pltpu.VMEM((U, W, W), F32)],
        compiler_params=pltpu.CompilerParams(dimension_semantics=("arbitrary", "arbitrary")),
    )(r, lw, k, v, a, b, jnp.asarray(bd), jnp.asarray(code), tri, jnp.asarray(eye))
    ng = nu * U
    st = st.reshape(n_seq, ng, G, head, G, head)
    st = jnp.stack([st[:, :, h, :, h, :] for h in range(G)], axis=2)
    return y, jnp.swapaxes(st, -1, -2).reshape(n_seq, ng * G, head, head)


def _wkv_step_kernel(r_ref, lw_ref, k_ref, v_ref, a_ref, b_ref, s_ref, *rest, heads, head):
    y_ref, so_ref = rest[-2:]
    for h in range(heads):
        w, a, b, k, r, v = (ref[h] for ref in (lw_ref, a_ref, b_ref, k_ref, r_ref, v_ref))
        w = jnp.exp(w)
        ys = []
        for i in range(head):
            s = s_ref[h, i]
            sa = jnp.sum(s * a, axis=0, keepdims=True)
            s_new = s * w + sa * b + v[i:i + 1, :] * k
            so_ref[h, i] = s_new
            ys.append(jnp.sum(s_new * r, axis=0, keepdims=True))
        y_ref[h] = jnp.concatenate(ys, axis=0)


def _wkv_step(r, lw, k, v, a, b, state, layer, prev_out):
    H, N, DB = r.shape
    hb = WKV_STEP_HEADS
    assert H % hb == 0
    vec = pl.BlockSpec((hb, N, DB), lambda g: (g, 0, 0))
    mat = pl.BlockSpec((None, hb, N, N, DB), lambda g: (layer, g, 0, 0, 0))
    in_specs = [vec] * 6 + [mat]
    args = [r, lw, k, v, a, b, state]
    aliases = {}
    if prev_out is not None:
        in_specs.append(pl.BlockSpec(memory_space=pl.ANY))
        args.append(prev_out)
        aliases = {7: 1}
    return pl.pallas_call(
        functools.partial(_wkv_step_kernel, heads=hb, head=N),
        grid=(H // hb,),
        in_specs=in_specs,
        out_specs=[vec, mat],
        out_shape=[jax.ShapeDtypeStruct((H, N, DB), F32), jax.ShapeDtypeStruct(state.shape, F32)],
        input_output_aliases=aliases,
        compiler_params=pltpu.CompilerParams(dimension_semantics=("arbitrary",), vmem_limit_bytes=VMEM_LIMIT),
    )(*args)


def _rwkv_pre_kernel(*refs, head, has_vmix):
    if has_vmix:
        (k_ref, wl_ref, al_ref, v_ref, vf_ref, vl_ref, w0_ref, a0_ref, kk_ref, ka_ref, v0_ref, bd_ref,
         lw_ref, k2_ref, sa_ref, sb_ref, v2_ref) = refs
    else:
        (k_ref, wl_ref, al_ref, w0_ref, a0_ref, kk_ref, ka_ref, bd_ref, lw_ref, k2_ref, sa_ref, sb_ref) = refs
    W = MXU_DIM
    ones_bd = bd_ref[...]
    for c in range(0, k_ref.shape[1], W):
        lanes = slice(c, c + W)
        k = k_ref[:, lanes]
        w_log = -jax.nn.softplus(-(w0_ref[:, lanes] + wl_ref[:, lanes])) - 0.5
        lw_ref[:, lanes] = -jnp.exp(w_log)
        a = jax.nn.sigmoid(a0_ref[:, lanes] + al_ref[:, lanes])
        kk = k * kk_ref[:, lanes]
        kk = kk / jnp.maximum(jnp.sqrt(_head_sum(kk * kk, ones_bd)), 1e-12)
        k2_ref[:, lanes] = k * (1.0 + (a - 1.0) * ka_ref[:, lanes])
        sa_ref[:, lanes] = -kk
        sb_ref[:, lanes] = kk * a
        if has_vmix:
            v = v_ref[:, lanes]
            v2_ref[:, lanes] = v + (vf_ref[:, lanes] - v) * jax.nn.sigmoid(v0_ref[:, lanes] + vl_ref[:, lanes])


def _rwkv_pre(k, wl, al, w0, a0, k_k, k_a, head, vmix=None):
    M, D = k.shape
    W = MXU_DIM
    tm = _pick(M, (256, 128) if vmix is None else (128,))
    idx = np.arange(W)
    ones_bd = jnp.asarray((idx[:, None] // head) == (idx[None, :] // head), BF16)
    row = pl.BlockSpec((tm, D), lambda i: (i, 0))
    vec = pl.BlockSpec((1, D), lambda i: (0, 0))
    bd = pl.BlockSpec((W, W), lambda i: (0, 0))
    as_row = lambda p: p.reshape(1, D)
    if vmix is None:
        args = [k, wl, al, as_row(w0), as_row(a0), as_row(k_k), as_row(k_a), ones_bd]
        in_specs = [row] * 3 + [vec] * 4 + [bd]
        n_out = 4
    else:
        v, v_first, vl, v0 = vmix
        args = [k, wl, al, v, v_first, vl, as_row(w0), as_row(a0), as_row(k_k), as_row(k_a), as_row(v0), ones_bd]
        in_specs = [row] * 6 + [vec] * 5 + [bd]
        n_out = 5
    return pl.pallas_call(
        functools.partial(_rwkv_pre_kernel, head=head, has_vmix=vmix is not None),
        grid=(M // tm,),
        in_specs=in_specs,
        out_specs=[row] * n_out,
        out_shape=[jax.ShapeDtypeStruct((M, D), F32)] * n_out,
        compiler_params=pltpu.CompilerParams(dimension_semantics=("arbitrary",), vmem_limit_bytes=VMEM_LIMIT),
    )(*args)


def _head_sum(x, ones_bd):
    hi = x.astype(BF16)
    lo = (x - hi.astype(F32)).astype(BF16)
    return _nn(hi, ones_bd) + _nn(lo, ones_bd)


def _rwkv_out_kernel(y_ref, r_ref, k_ref, v_ref, g_ref, lw_ref, lb_ref, rk_ref, bd_ref, o_ref, *, head, eps):
    W = MXU_DIM
    ones_bd = bd_ref[...]
    inv_n = 1.0 / head
    for c in range(0, y_ref.shape[1], W):
        lanes = slice(c, c + W)
        y = y_ref[:, lanes]
        d = y - _head_sum(y, ones_bd) * inv_n
        var = _head_sum(d * d, ones_bd) * inv_n
        yn = d * lax.rsqrt(var + eps) * lw_ref[:, lanes] + lb_ref[:, lanes]
        bonus = _head_sum(r_ref[:, lanes] * k_ref[:, lanes] * rk_ref[:, lanes], ones_bd) * v_ref[:, lanes]
        o_ref[:, lanes] = ((yn + bonus) * g_ref[:, lanes]).astype(o_ref.dtype)


def _rwkv_out(y, r, k, v, g, lnx_w, lnx_b, r_k, head):
    M, D = y.shape
    W = MXU_DIM
    tm = _pick(M, (256, 128))
    idx = np.arange(W)
    ones_bd = jnp.asarray((idx[:, None] // head) == (idx[None, :] // head), BF16)
    row = pl.BlockSpec((tm, D), lambda i: (i, 0))
    vec = pl.BlockSpec((1, D), lambda i: (0, 0))
    return pl.pallas_call(
        functools.partial(_rwkv_out_kernel, head=head, eps=head * GN_EPS_PER_CHANNEL),
        grid=(M // tm,),
        in_specs=[row] * 5 + [vec] * 3 + [pl.BlockSpec((W, W), lambda i: (0, 0))],
        out_specs=row,
        out_shape=jax.ShapeDtypeStruct((M, D), BF16),
        compiler_params=pltpu.CompilerParams(dimension_semantics=("arbitrary",), vmem_limit_bytes=VMEM_LIMIT),
    )(y, r, k, v, g, lnx_w.reshape(1, D), lnx_b.reshape(1, D), r_k.reshape(1, D), ones_bd)


def _split_dot(x, w):
    hi = x.astype(BF16)
    lo = (x - hi.astype(F32)).astype(BF16)
    return _nn(hi, w) + _nn(lo, w)


def _qk_prep_kernel(*refs, qk_dim, rope):
    if rope:
        (n_ref, x1_ref, x2_ref, c_ref, s_ref, rs_ref, gn_ref, g1_ref, g2_ref, sn_ref, sr_ref, snt_ref, srt_ref,
         on_ref, o1_ref, o2_ref, ri_ref) = refs
        x1, x2, c, s = x1_ref[...], x2_ref[...], c_ref[...], s_ref[...]
        r1 = x1 * c - x2 * s
        r2 = x2 * c + x1 * s
    else:
        (n_ref, x1_ref, x2_ref, rs_ref, gn_ref, g1_ref, g2_ref, sn_ref, sr_ref, snt_ref, srt_ref,
         on_ref, o1_ref, o2_ref, ri_ref) = refs
        r1, r2 = x1_ref[...], x2_ref[...]
    n = n_ref[...]
    ssq = _split_dot(n * n, sn_ref[...]) + _split_dot(r1 * r1 + r2 * r2, sr_ref[...])
    rinv = lax.rsqrt(ssq * (1.0 / qk_dim) + RMS_EPS)
    ri_ref[...] = rinv
    rinv = rinv * rs_ref[...]
    rn = _split_dot(rinv, snt_ref[...])
    rr = _split_dot(rinv, srt_ref[...])
    on_ref[...] = (n * rn * gn_ref[...]).astype(on_ref.dtype)
    o1_ref[...] = (r1 * rr * g1_ref[...]).astype(o1_ref.dtype)
    o2_ref[...] = (r2 * rr * g2_ref[...]).astype(o2_ref.dtype)


def _qk_prep(src_n, col_n, src_1, col_1, src_2, col_2, heads, nope, half, gain, row_scale, cos_t=None, sin_t=None):
    M = src_n.shape[0]
    wn, wr = heads * nope, heads * half
    tm = _pick(M, (256, 128))
    seg = lambda width, per: jnp.asarray(
        (np.arange(width)[:, None] // per) == np.arange(LANES)[None, :], BF16)
    sn, sr = seg(wn, nope), seg(wr, half)
    gn = jnp.tile(gain[:nope], heads).reshape(1, wn)
    gr = jnp.tile(gain[nope:], heads).reshape(1, wr)
    rows = lambda width, col: pl.BlockSpec((tm, width), lambda i: (i, col))
    full = lambda shape: pl.BlockSpec(shape, lambda i: (0, 0))
    rope = cos_t is not None
    args = [src_n, src_1, src_2] + ([cos_t, sin_t] if rope else []) + [row_scale, gn, gr, gr, sn, sr, sn.T, sr.T]
    in_specs = ([rows(wn, col_n), rows(wr, col_1), rows(wr, col_2)] + ([rows(wr, 0)] * 2 if rope else [])
                + [rows(1, 0), full((1, wn)), full((1, wr)), full((1, wr)),
                   full((wn, LANES)), full((wr, LANES)), full((LANES, wn)), full((LANES, wr))])
    return pl.pallas_call(
        functools.partial(_qk_prep_kernel, qk_dim=nope + 2 * half, rope=rope),
        grid=(M // tm,),
        in_specs=in_specs,
        out_specs=[rows(wn, 0), rows(wr, 0), rows(wr, 0), rows(LANES, 0)],
        out_shape=[jax.ShapeDtypeStruct((M, wn), BF16), jax.ShapeDtypeStruct((M, wr), BF16),
                   jax.ShapeDtypeStruct((M, wr), BF16), jax.ShapeDtypeStruct((M, LANES), F32)],
        compiler_params=pltpu.CompilerParams(dimension_semantics=("arbitrary",), vmem_limit_bytes=VMEM_LIMIT),
    )(*args)


def _flash_kernel(qi_ref, ki_ref, flag_ref, q_ref, k_ref, v_ref, o_ref, m_sc, acc_sc, *, tq, tk, vdim, heads):
    p = pl.program_id(2)
    qi = qi_ref[p]
    ki = ki_ref[p]
    flags = flag_ref[p]
    hs = range(heads)

    @pl.when(ki == 0)
    def _():
        m_sc[...] = jnp.full_like(m_sc, -jnp.inf)
        acc_sc[...] = jnp.zeros_like(acc_sc)

    def update(masked):
        s = [_nt(q_ref[0, h], k_ref[0, h]) for h in hs]
        if masked:
            qpos = qi * tq + lax.broadcasted_iota(jnp.int32, (tq, 1), 0)
            kpos = ki * tk + lax.broadcasted_iota(jnp.int32, (1, tk), 1)
            keep = kpos <= qpos
            s = [jnp.where(keep, x, -jnp.inf) for x in s]
        m_old = [m_sc[h] for h in hs]
        m_new = [jnp.maximum(mo, jnp.max(x, axis=-1, keepdims=True)) for mo, x in zip(m_old, s)]
        pm = [jnp.exp2(x - mn).astype(BF16) for x, mn in zip(s, m_new)]
        for h in hs:
            acc_sc[h] = jnp.exp2(m_old[h] - m_new[h]) * acc_sc[h] + _nn(pm[h], v_ref[0, h])
            m_sc[h] = m_new[h]

    pl.when((flags & 2) != 0)(lambda: update(True))
    pl.when((flags & 2) == 0)(lambda: update(False))

    @pl.when((flags & 1) != 0)
    def _():
        outs = []
        for h in hs:
            acc = acc_sc[h]
            outs.append(acc[:, :vdim] / acc[:, vdim:vdim + 1])
        o_ref[0] = jnp.concatenate(outs, axis=1).astype(o_ref.dtype)


def _flash(q, k, v, vdim, tq, tk):
    B, H, T, E = q.shape
    VA = v.shape[-1]
    HS = FLASH_HEADS
    assert H % HS == 0
    pairs = [(qi, ki) for qi in range(T // tq) for ki in range(T // tk) if ki * tk <= qi * tq + tq - 1]
    n = len(pairs)
    qi_tab = np.array([p[0] for p in pairs], np.int32)
    ki_tab = np.array([p[1] for p in pairs], np.int32)
    flags = np.array([(1 if (i + 1 == n or pairs[i + 1][0] != pairs[i][0]) else 0)
                      + (2 if (ki + 1) * tk - 1 > qi * tq else 0)
                      for i, (qi, ki) in enumerate(pairs)], np.int32)
    grid_spec = pltpu.PrefetchScalarGridSpec(
        num_scalar_prefetch=3,
        grid=(B, H // HS, n),
        in_specs=[pl.BlockSpec((1, HS, tq, E), lambda b, h, p, qt, kt, ft: (b, h, qt[p], 0)),
                  pl.BlockSpec((1, HS, tk, E), lambda b, h, p, qt, kt, ft: (b, h, kt[p], 0)),
                  pl.BlockSpec((1, HS, tk, VA), lambda b, h, p, qt, kt, ft: (b, h, kt[p], 0))],
        out_specs=pl.BlockSpec((1, tq, HS * vdim), lambda b, h, p, qt, kt, ft: (b, qt[p], h)),
        scratch_shapes=[pltpu.VMEM((HS, tq, 1), F32), pltpu.VMEM((HS, tq, VA), F32)],
    )
    return pl.pallas_call(
        functools.partial(_flash_kernel, tq=tq, tk=tk, vdim=vdim, heads=HS),
        grid_spec=grid_spec,
        out_shape=jax.ShapeDtypeStruct((B, T, H * vdim), BF16),
        compiler_params=pltpu.CompilerParams(
            dimension_semantics=("arbitrary", "arbitrary", "arbitrary"), vmem_limit_bytes=VMEM_LIMIT),
    )(jnp.asarray(qi_tab), jnp.asarray(ki_tab), jnp.asarray(flags), q, k, v)


def _paged_kernel(pt_ref, *refs, heads, nope, qk_dim, n_steps, pages, have_rinv):
    del pt_ref
    c_refs, p_refs = refs[:pages], refs[pages:2 * pages]
    qa_ref, qp_ref, x_ref = refs[2 * pages:2 * pages + 3]
    rest = refs[2 * pages + 3:]
    if have_rinv:
        acc_ref, m_ref, l_ref, m_sc, l_sc, acc_sc = rest
    else:
        acc_ref, m_ref, l_ref, rinv_ref, m_sc, l_sc, acc_sc, lhs_sc = rest
    step = pl.program_id(1)
    rows_w = heads * nope

    @pl.when(step == 0)
    def _():
        m_sc[...] = jnp.full_like(m_sc, -jnp.inf)
        l_sc[...] = jnp.zeros_like(l_sc)
        acc_sc[...] = jnp.zeros_like(acc_sc)
        if not have_rinv:
            lhs_sc[rows_w:rows_w + heads, :] = qa_ref[0].astype(BF16)

    if not have_rinv:
        @pl.when((step == 0) & (pl.program_id(0) == 0))
        def _():
            lhs_sc[0:rows_w, :] = x_ref[...]

    c = jnp.concatenate([r[0] for r in c_refs], axis=0).astype(BF16)
    kp = jnp.concatenate([r[0] for r in p_refs], axis=1)
    tokens = c.shape[0]
    if have_rinv:
        rinv = jnp.concatenate([x_ref[0, j] for j in range(x_ref.shape[1])], axis=1)
        s_nope = _nt(qa_ref[0].astype(BF16), c)
    else:
        sub = MXU_DIM
        parts, scores = [], []
        for t in range(0, tokens, sub):
            knt = _nt(lhs_sc[...], c[t:t + sub])
            parts.append(jnp.sum((knt[:rows_w] * knt[:rows_w]).reshape(heads, nope, sub), axis=1))
            scores.append(knt[rows_w:])
        ssq = jnp.concatenate(parts, axis=1)
        s_nope = jnp.concatenate(scores, axis=1)
        kss = jnp.sum(kp * kp, axis=0, keepdims=True)
        rinv = lax.rsqrt((ssq + kss) * (1.0 / qk_dim) + RMS_EPS)
        rinv_ref[0, 0] = rinv
    s = (s_nope + _nn(qp_ref[0].astype(BF16), kp.astype(BF16))) * rinv
    m_old = m_sc[...]
    m_new = jnp.maximum(m_old, jnp.max(s, axis=-1, keepdims=True))
    alpha = jnp.exp(m_old - m_new)
    pm = jnp.exp(s - m_new)
    l_sc[...] = alpha * l_sc[...] + jnp.sum(pm, axis=-1, keepdims=True)
    acc_sc[...] = alpha * acc_sc[...] + _nn(pm.astype(BF16), c)
    m_sc[...] = m_new

    @pl.when(step == n_steps - 1)
    def _():
        acc_ref[0] = acc_sc[...]
        m_ref[0] = jnp.broadcast_to(m_sc[...], m_ref.shape[1:])
        l_ref[0] = jnp.broadcast_to(l_sc[...], l_ref.shape[1:])


def _paged_attn(page_table, cache_ckv, cache_kpe_t, qa, qp, nope, *, wnt=None, rinv=None):
    DB, n_pages = page_table.shape
    _, page, R = cache_ckv.shape
    rope = cache_kpe_t.shape[1]
    H = qa.shape[1]
    have_rinv = rinv is not None
    P = PAGES_PER_STEP * (PAGE_STEP_RATIO if have_rinv else 1)
    assert n_pages % P == 0
    n_steps = n_pages // P
    page_spec = lambda shape, i: pl.BlockSpec((1,) + shape, lambda b, s, pt: (pt[b, P * s + i], 0, 0))
    if have_rinv:
        rinv_spec = pl.BlockSpec((1, PAGE_STEP_RATIO, H, PAGES_PER_STEP * page), lambda b, s, pt: (b, s, 0, 0))
    else:
        rinv_spec = pl.BlockSpec((1, 1, H, P * page), lambda b, s, pt: (b, s, 0, 0))
    x_spec = rinv_spec if have_rinv else pl.BlockSpec(wnt.shape, lambda b, s, pt: (0, 0))
    stat_spec = pl.BlockSpec((1, H, LANES), lambda b, s, pt: (b, 0, 0))
    out_specs = [pl.BlockSpec((1, H, R), lambda b, s, pt: (b, 0, 0)), stat_spec, stat_spec]
    out_shape = [jax.ShapeDtypeStruct((DB, H, R), F32),
                 jax.ShapeDtypeStruct((DB, H, LANES), F32),
                 jax.ShapeDtypeStruct((DB, H, LANES), F32)]
    scratch = [pltpu.VMEM((H, 1), F32), pltpu.VMEM((H, 1), F32), pltpu.VMEM((H, R), F32)]
    if not have_rinv:
        out_specs.append(rinv_spec)
        out_shape.append(jax.ShapeDtypeStruct((DB, n_steps, H, P * page), F32))
        scratch.append(pltpu.VMEM((H * nope + H, R), BF16))
    grid_spec = pltpu.PrefetchScalarGridSpec(
        num_scalar_prefetch=1,
        grid=(DB, n_steps),
        in_specs=([page_spec((page, R), i) for i in range(P)] + [page_spec((rope, page), i) for i in range(P)]
                  + [pl.BlockSpec((1, H, R), lambda b, s, pt: (b, 0, 0)),
                     pl.BlockSpec((1, H, rope), lambda b, s, pt: (b, 0, 0)), x_spec]),
        out_specs=out_specs,
        scratch_shapes=scratch,
    )
    outs = pl.pallas_call(
        functools.partial(_paged_kernel, heads=H, nope=nope, qk_dim=nope + rope, n_steps=n_steps, pages=P,
                          have_rinv=have_rinv),
        grid_spec=grid_spec,
        out_shape=out_shape,
        compiler_params=pltpu.CompilerParams(
            dimension_semantics=("arbitrary", "arbitrary"), vmem_limit_bytes=VMEM_LIMIT),
    )(page_table, *([cache_ckv] * P), *([cache_kpe_t] * P), qa, qp, rinv if have_rinv else wnt)
    acc, m, l = outs[:3]
    return acc, m[:, :, 0], l[:, :, 0], (rinv if have_rinv else outs[3])


def _pad_cols(w, n):
    return jnp.pad(w, ((0, 0), (0, n - w.shape[1])))


def _pad_rows(w, n):
    return jnp.pad(w, ((0, n - w.shape[0]), (0, 0)))


def _lora(x, w1, w2, act):
    rank = w1.shape[1]
    rp = -(-rank // LANES) * LANES
    mid = _mm(x, _pad_cols(w1, rp), act=act, out_dtype=BF16)
    return _mm(mid, _pad_rows(w2, rp))


def _rope(x, cos, sin):
    half = x.shape[-1] // 2
    x1, x2 = x[..., :half], x[..., half:]
    return jnp.concatenate([x1 * cos - x2 * sin, x2 * cos + x1 * sin], axis=-1)


def kernel(x_prompt, x_sample, state_shift, state_wkv, state_conv, cache_ckv, cache_kpe, page_table, meta_tokens, norm_mix, norm_ffn, mu, w_rkv, w_o_a, w0, w1, w2, a0, a1, a2, v0, v1, v2, g1, g2, k_k, k_a, r_k, lnx_w, lnx_b, ffn_w_in, ffn_conv_w, ffn_conv_b, ffn_w_out, norm_kv, w_dkv, g_ckv, w_ukv, g_k, w_dq, g_q, w_uq, g_qn, w_o_b):
    B, seq, D = x_prompt.shape
    DB = x_sample.shape[0]
    assert x_sample.shape[1] == 1
    n_meta = meta_tokens.shape[0]
    depth = norm_mix.shape[0]
    n_a = mu.shape[0]
    HA, NA = r_k.shape[1], r_k.shape[2]
    assert ffn_conv_w.shape[1] == 3
    R = g_ckv.shape[0]
    rope = w_dkv.shape[1] - R
    HB = w_ukv.shape[1]
    nope = g_k.shape[0] - rope // 2
    vdim = w_ukv.shape[2] - nope
    qk = nope + rope
    T = seq + n_meta
    Tp = -(-T // ROW_ALIGN) * ROW_ALIGN
    Tpa = -(-Tp // ATTN_ALIGN) * ATTN_ALIGN
    MP = B * Tp
    M = MP + DB
    past_len = page_table.shape[1] * cache_ckv.shape[1]
    scale = qk ** -0.5

    h0 = jnp.concatenate([jnp.broadcast_to(meta_tokens[None], (B, n_meta, D)), x_prompt], axis=1)
    h0 = jnp.pad(h0, ((0, 0), (0, Tp - T), (0, 0)))
    h = jnp.concatenate([h0.reshape(MP, D), x_sample.reshape(DB, D)], axis=0)

    t_of_row = jnp.concatenate([jnp.tile(jnp.arange(Tp), B), jnp.full((DB,), past_len)])
    inv_freq = ROPE_THETA ** (-jnp.arange(0, rope, 2, dtype=F32) / rope)
    ang = t_of_row.astype(F32)[:, None] * inv_freq[None]
    cos, sin = jnp.cos(ang), jnp.sin(ang)
    last_rows = np.array([b * Tp + T - 1 for b in range(B)])

    state_wkv_t = jnp.transpose(state_wkv, (0, 2, 3, 4, 1))
    cache_kpe_t = jnp.swapaxes(cache_kpe, 1, 2)
    w_rkv3 = w_rkv.reshape(n_a * 3, D, D)
    half = rope // 2
    wn, wr = HB * nope, HB * half
    uq_part = lambda lo, hi: w_uq[..., lo:hi].reshape(w_uq.shape[0], w_uq.shape[1], HB * (hi - lo))
    w_uq3 = jnp.concatenate([uq_part(0, nope), uq_part(nope, nope + half), uq_part(nope + half, qk)], axis=-1)
    cos_t, sin_t = jnp.tile(cos, (1, HB)), jnp.tile(sin, (1, HB))
    q_row_scale = jnp.concatenate([jnp.full((MP, 1), scale * LOG2E, F32), jnp.full((DB, 1), scale, F32)])

    def to_heads(parts):
        t = jnp.concatenate([p[:MP].reshape(B, Tp, HB, -1) for p in parts], axis=-1)
        return jnp.pad(t.transpose(0, 2, 1, 3), ((0, 0), (0, 0), (0, Tpa - Tp), (0, 0)))
    w_ob3 = w_o_b.reshape(w_o_b.shape[0], HB * vdim, D)

    shift_p, shift_s, wkv_p, conv_p, conv_s = [], [], [], [], []
    wkv_s_t = None
    v_first = None
    kv_p = kv_s = None
    ckv = kpe = None

    for i in range(depth):
        if i < n_a:
            xn, xs = _norm_mix(h, norm_mix[i], mu[i], state_shift[i], B, Tp)
            shift_p.append(xn[last_rows])
            shift_s.append(xn[MP:])
            r = _mm(xs[0], w_rkv3, layer=3 * i)
            k = _mm(xs[1], w_rkv3, layer=3 * i + 1)
            v = _mm(xs[2], w_rkv3, layer=3 * i + 2)
            wl = _lora(xs[3], w1[i], w2[i], jnp.tanh)
            al = _lora(xs[4], a1[i], a2[i], None)
            g = _lora(xs[5], g1[i], g2[i], jax.nn.sigmoid)
            if i > 0:
                vl = _lora(xs[2], v1[i - 1], v2[i - 1], None)
                log_decay, k, sa, sb, v = _rwkv_pre(k, wl, al, w0[i], a0[i], k_k[i], k_a[i], NA,
                                                    vmix=(v, v_first, vl, v0[i - 1]))
            else:
                v_first = v
                log_decay, k, sa, sb = _rwkv_pre(k, wl, al, w0[i], a0[i], k_k[i], k_a[i], NA)
            y, st_p = _wkv_chunked(r, log_decay, k, v, sa, sb, B, Tp, T, NA)
            lanes_t = lambda t: t[MP:].reshape(DB, HA, NA).transpose(1, 2, 0)
            y_s, wkv_s_t = _wkv_step(*(lanes_t(t) for t in (r, log_decay, k, v, sa, sb)), state_wkv_t, i, wkv_s_t)
            y = lax.dynamic_update_slice(y, y_s.transpose(2, 0, 1).reshape(DB, D), (MP, 0))
            wkv_p.append(st_p)
            gated = _rwkv_out(y, r, k, v, g, lnx_w[i], lnx_b[i], r_k[i], NA)
            h = _mm(gated, w_o_a, layer=i, residual=h)
        else:
            j = i - n_a
            xn = _rmsnorm(h, norm_mix[i], BF16)
            cq = _rmsnorm(_mm(xn, w_dq, layer=j), g_q[j], BF16)
            q2d = _mm(cq, w_uq3, layer=j)
            q_parts = _qk_prep(q2d, 0, q2d, wn // wr, q2d, wn // wr + 1, HB, nope, half, g_qn[j], q_row_scale,
                               cos_t, sin_t)[:3]
            o_p = _flash(to_heads(q_parts), kv_p[0], kv_p[1], vdim,
                         tq=Tpa // FLASH_BLOCKS, tk=Tpa // FLASH_BLOCKS)[:, :Tp]
            qn_s, q1_s, q2_s = (t[MP:].astype(F32).reshape(DB, HB, -1) for t in q_parts)
            qn = qn_s * g_k[:nope]
            qp = jnp.concatenate([q1_s * g_k[nope:], q2_s * g_k[nope:]], axis=-1)
            qa = _bmm(qn.transpose(1, 0, 2), kv_s['wnt3']).transpose(1, 0, 2)
            acc, m, l, kv_s['rinv_cache'] = _paged_attn(
                page_table, cache_ckv, cache_kpe_t, qa, qp, nope, wnt=kv_s['wnt'], rinv=kv_s['rinv_cache'])
            c_new, kp_new = kv_s['ckv'], kv_s['kpe']
            s_new = kv_s['rinv'] * (jnp.sum(qa * c_new[:, None, :], axis=-1) + jnp.sum(qp * kp_new[:, None, :], axis=-1))
            m_f = jnp.maximum(m, s_new)
            alpha = jnp.exp(m - m_f)
            pn = jnp.exp(s_new - m_f)
            l_f = l * alpha + pn
            ctx = (acc * alpha[..., None] + pn[..., None] * c_new[:, None, :]) / l_f[..., None]
            o_s = _bmm(ctx.transpose(1, 0, 2), kv_s['wv']).transpose(1, 0, 2).reshape(DB, HB * vdim)
            attn = jnp.concatenate([o_p.reshape(MP, HB * vdim), o_s.astype(BF16)], axis=0)
            h = _mm(attn, w_ob3, layer=j, residual=h)

        xn = _rmsnorm(h, norm_ffn[i], BF16)
        gated, c_tail, c_s = _ffn_in(xn, ffn_w_in, i, ffn_conv_w[i], ffn_conv_b[i], state_conv[i], B, Tp, T)
        conv_p.append(c_tail)
        conv_s.append(jnp.stack([state_conv[i][:, 1], c_s], axis=1))
        h = _mm(gated, ffn_w_out, layer=i, residual=h)

        if i == n_a - 1:
            xk = _rmsnorm(h, norm_kv, BF16)
            ckv = _rmsnorm(_mm(xk, w_dkv[:, :R]), g_ckv, F32)
            kpe = _rope(_mm(xk, w_dkv[:, R:]), cos, sin)
            w_ukv2 = jnp.concatenate([w_ukv[:, :, :nope].reshape(R, wn), w_ukv[:, :, nope:].reshape(R, HB * vdim)],
                                     axis=1)
            kv2d = _mm(ckv.astype(BF16), w_ukv2)
            kp1 = jnp.tile(kpe[:, :half], (1, HB))
            kp2 = jnp.tile(kpe[:, half:], (1, HB))
            *k_parts, rinv = _qk_prep(kv2d, 0, kp1, 0, kp2, 0, HB, nope, half, g_k, jnp.ones((M, 1), F32))
            rinv = rinv[:, :HB]
            v3 = kv2d[:MP, wn:].reshape(MP, HB, vdim).astype(BF16)
            v_aug = jnp.concatenate([v3, jnp.ones((MP, HB, 1), BF16), jnp.zeros((MP, HB, vdim - 1), BF16)], axis=-1)
            kv_p = (to_heads(k_parts), to_heads([v_aug.reshape(MP, HB * 2 * vdim)]))
            wnt3 = w_ukv[:, :, :nope].transpose(1, 2, 0)
            kv_s = dict(ckv=ckv[MP:], kpe=kpe[MP:], rinv=rinv[MP:], wnt3=wnt3, rinv_cache=None,
                        wnt=wnt3.reshape(HB * nope, R).astype(BF16),
                        wv=w_ukv[:, :, nope:].transpose(1, 0, 2))

    wkv_s = jnp.transpose(wkv_s_t, (0, 4, 1, 2, 3))
    return (h[:MP].reshape(B, Tp, D)[:, n_meta:T], h[MP:].reshape(DB, 1, D),
            jnp.stack(shift_p), jnp.stack(wkv_p), jnp.stack(conv_p),
            ckv[:MP].reshape(B, Tp, R)[:, :T], kpe[:MP].reshape(B, Tp, rope)[:, :T],
            jnp.stack(shift_s), wkv_s, jnp.stack(conv_s),
            ckv[MP:].reshape(DB, 1, R), kpe[MP:].reshape(DB, 1, rope))
```
